```python
import math
import jax, jax.numpy as jnp
from jax import lax
import numpy as np

D_MODEL = 1024
BATCH = 8
SEQ = 2048
DEPTH = 4
DEC_BATCH = 128
DEC_SEQ = 1
PAST_LEN = 16384
PAGE_SIZE = 128

N_MIXERS = 3
N_META = 16
EPS = 1e-6
LN_EPS = 1e-5
SSD_EXPAND = 2
D_INNER = SSD_EXPAND * D_MODEL
SSD_HEAD_DIM = 64
SSD_HEADS = D_INNER // SSD_HEAD_DIM
SSD_D_STATE = 128
SSD_GROUPS = 8
SSD_CONV = 4
SSD_CHUNK = 256
SSD_CONV_DIM = D_INNER + 2 * SSD_GROUPS * SSD_D_STATE
SSD_IN_DIM = D_INNER + SSD_CONV_DIM + SSD_HEADS
SC_WIDTH = 3
CF_WIDTH = 31
CF_DIM = D_MODEL
D_FF = ((8 * D_MODEL // 3 + 255) // 256) * 256
N_SSD = (DEPTH + 2) // 3
N_SC = (DEPTH + 1) // 3
N_CF = DEPTH // 3

kernel_name = "hybrid_ssd_shortconv_conformer_step"


def rmsnorm(x, g):
    xf = x.astype(jnp.float32)
    y = xf * lax.rsqrt(jnp.mean(xf * xf, axis=-1, keepdims=True) + EPS) * g.astype(jnp.float32)
    return y.astype(x.dtype)


def layernorm(x, g, b):
    xf = x.astype(jnp.float32)
    mu = jnp.mean(xf, axis=-1, keepdims=True)
    var = jnp.mean(jnp.square(xf - mu), axis=-1, keepdims=True)
    y = (xf - mu) * lax.rsqrt(var + LN_EPS) * g.astype(jnp.float32) + b.astype(jnp.float32)
    return y.astype(x.dtype)


def causal_dwconv(x, buf, w, bias=None):
    L, C = x.shape[1], x.shape[2]
    xp = jnp.concatenate([buf.astype(x.dtype), x], axis=1)
    y = lax.conv_general_dilated(xp, w.astype(x.dtype)[:, None, :], window_strides=(1,), padding='VALID',
                                 dimension_numbers=('NWC', 'WIO', 'NWC'), feature_group_count=C)
    if bias is not None:
        y = y + bias.astype(x.dtype)
    return y, xp[:, L:]


def ssd_scan(x, dt, A, Bm, Cm, chunk, init_state):
    f32 = jnp.float32
    b, l, h, p = x.shape
    g, n = Bm.shape[-2], Bm.shape[-1]
    j = h // g
    c = l // chunk
    xd = (x.astype(f32) * dt[..., None]).reshape(b, c, chunk, g, j, p)
    dA = jnp.moveaxis((dt * A).reshape(b, c, chunk, g, j), 2, -1)
    cs = jnp.cumsum(dA, axis=-1)
    Bc = Bm.astype(f32).reshape(b, c, chunk, g, n)
    Cc = Cm.astype(f32).reshape(b, c, chunk, g, n)
    causal = jnp.tril(jnp.ones((chunk, chunk), dtype=bool))
    decay_in = jnp.exp(jnp.where(causal, cs[..., :, None] - cs[..., None, :], -jnp.inf))
    cb = jnp.einsum('bctgn,bcsgn->bcgts', Cc, Bc)
    y_diag = jnp.einsum('bcgjts,bcsgjp->bctgjp', cb[:, :, :, None] * decay_in, xd)
    decay_to_end = jnp.exp(cs[..., -1:] - cs)
    chunk_states = jnp.einsum('bctgn,bcgjt,bctgjp->bcgjpn', Bc, decay_to_end, xd)
    chunk_decay = jnp.exp(cs[..., -1])

    def step(s, inp):
        st, dec = inp
        return s * dec[..., None, None] + st, s

    s0 = init_state.astype(f32).reshape(b, g, j, p, n)
    final, starts = lax.scan(step, s0, (jnp.moveaxis(chunk_states, 1, 0), jnp.moveaxis(chunk_decay, 1, 0)))
    y_off = jnp.einsum('bctgn,cbgjpn,bcgjt->bctgjp', Cc, starts, jnp.exp(cs))
    y = (y_diag + y_off).reshape(b, l, h, p)
    return y, final.reshape(b, h, p, n)


def mamba2_mixer(h, conv_buf, ssm_state, segments, w_in, conv_w, conv_b, dt_bias, a_log, d_skip, norm_w, w_out):
    f32 = jnp.float32
    b, l, _ = h.shape
    zxbcdt = h @ w_in
    z = zxbcdt[..., :D_INNER]
    xbc = zxbcdt[..., D_INNER:D_INNER + SSD_CONV_DIM]
    dt_raw = zxbcdt[..., D_INNER + SSD_CONV_DIM:]
    xbc, new_conv = causal_dwconv(xbc, conv_buf, conv_w, conv_b)
    xbc = jax.nn.silu(xbc)
    GN = SSD_GROUPS * SSD_D_STATE
    xs = xbc[..., :D_INNER].reshape(b, l, SSD_HEADS, SSD_HEAD_DIM)
    Bm = xbc[..., D_INNER:D_INNER + GN].reshape(b, l, SSD_GROUPS, SSD_D_STATE)
    Cm = xbc[..., D_INNER + GN:].reshape(b, l, SSD_GROUPS, SSD_D_STATE)
    dt = jax.nn.softplus(dt_raw.astype(f32) + dt_bias.astype(f32))
    A = -jnp.exp(a_log.astype(f32))
    ys = []
    state = ssm_state
    start = 0
    for seg_len, chunk in segments:
        sl = slice(start, start + seg_len)
        y_seg, state = ssd_scan(xs[:, sl], dt[:, sl], A, Bm[:, sl], Cm[:, sl], chunk, state)
        ys.append(y_seg)
        start += seg_len
    y = jnp.concatenate(ys, axis=1) + xs.astype(f32) * d_skip.astype(f32)[:, None]
    y = y.reshape(b, l, D_INNER) * jax.nn.silu(z.astype(f32))
    yg = y.reshape(b, l, SSD_GROUPS, D_INNER // SSD_GROUPS)
    yg = yg * lax.rsqrt(jnp.mean(yg * yg, axis=-1, keepdims=True) + EPS)
    y = yg.reshape(b, l, D_INNER) * norm_w.astype(f32)
    return y.astype(h.dtype) @ w_out, new_conv, state.astype(ssm_state.dtype)


def shortconv_mixer(h, buf, w_in, conv_w, w_out):
    bcx = h @ w_in
    bg = bcx[..., :D_MODEL]
    cg = bcx[..., D_MODEL:2 * D_MODEL]
    xv = bcx[..., 2 * D_MODEL:]
    v, new_buf = causal_dwconv(cg * xv, buf, conv_w)
    return (bg * v) @ w_out, new_buf


def conformer_conv_mixer(h, buf, w_pw1, b_pw1, dw_w, dw_b, ln_g, ln_b, w_pw2, b_pw2):
    a = h @ w_pw1 + b_pw1
    u = a[..., :CF_DIM] * jax.nn.sigmoid(a[..., CF_DIM:])
    v, new_buf = causal_dwconv(u, buf, dw_w, dw_b)
    v = jax.nn.silu(layernorm(v, ln_g, ln_b))
    return v @ w_pw2 + b_pw2, new_buf


def swiglu(h, w_gate, w_up, w_down):
    return (jax.nn.silu(h @ w_gate) * (h @ w_up)) @ w_down


def trunk(x, st_ssm, st_ssm_conv, st_sconv, st_cconv, segments, p):
    n_ssm, n_ssm_conv, n_sconv, n_cconv = [], [], [], []
    for i in range(DEPTH):
        kind = i % N_MIXERS
        j = i // N_MIXERS
        hn = rmsnorm(x, p['norm_mix'][i])
        if kind == 0:
            out, cb, st = mamba2_mixer(hn, st_ssm_conv[j], st_ssm[j], segments,
                                       p['ssd_w_in'][j], p['ssd_conv_w'][j], p['ssd_conv_b'][j],
                                       p['ssd_dt_bias'][j], p['ssd_a_log'][j], p['ssd_d'][j],
                                       p['ssd_norm_w'][j], p['ssd_w_out'][j])
            n_ssm_conv.append(cb)
            n_ssm.append(st)
        elif kind == 1:
            out, cb = shortconv_mixer(hn, st_sconv[j], p['sc_w_in'][j], p['sc_conv_w'][j], p['sc_w_out'][j])
            n_sconv.append(cb)
        else:
            out, cb = conformer_conv_mixer(hn, st_cconv[j], p['cf_w_pw1'][j], p['cf_b_pw1'][j],
                                           p['cf_dw_w'][j], p['cf_dw_b'][j], p['cf_ln_g'][j],
                                           p['cf_ln_b'][j], p['cf_w_pw2'][j], p['cf_b_pw2'][j])
            n_cconv.append(cb)
        x = x + out
        x = x + swiglu(rmsnorm(x, p['norm_ffn'][i]), p['ffn_w_gate'][i], p['ffn_w_up'][i], p['ffn_w_down'][i])
    x = rmsnorm(x, p['norm_final'])
    return x, jnp.stack(n_ssm), jnp.stack(n_ssm_conv), jnp.stack(n_sconv), jnp.stack(n_cconv)


def setup_inputs(seed: int = 0) -> dict:
    key = jax.random.key(seed)
    keys = iter(jax.random.split(key, 64))
    f32 = jnp.float32

    def nrm(shape, scale):
        return jax.random.normal(next(keys), shape, f32) * scale

    def gain(shape):
        return 1.0 + nrm(shape, 0.01)

    dt0 = jnp.exp(jax.random.uniform(next(keys), (N_SSD, SSD_HEADS), f32) * (math.log(0.1) - math.log(0.001)) + math.log(0.001))
    inp = {
        'x_prompt': nrm((BATCH, SEQ, D_MODEL), 1.0),
        'x_sample': nrm((DEC_BATCH, DEC_SEQ, D_MODEL), 1.0),
        'state_ssm': nrm((N_SSD, DEC_BATCH, SSD_HEADS, SSD_HEAD_DIM, SSD_D_STATE), 0.1),
        'state_ssm_conv': nrm((N_SSD, DEC_BATCH, SSD_CONV - 1, SSD_CONV_DIM), 1.0),
        'state_sconv': nrm((N_SC, DEC_BATCH, SC_WIDTH - 1, D_MODEL), 1.0),
        'state_cconv': nrm((N_CF, DEC_BATCH, CF_WIDTH - 1, CF_DIM), 1.0),
        'meta_tokens': nrm((N_META, D_MODEL), 1.0),
        'norm_mix': gain((DEPTH, D_MODEL)),
        'norm_ffn': gain((DEPTH, D_MODEL)),
        'norm_final': gain((D_MODEL,)),
        'ssd_w_in': nrm((N_SSD, D_MODEL, SSD_IN_DIM), D_MODEL ** -0.5),
        'ssd_conv_w': nrm((N_SSD, SSD_CONV, SSD_CONV_DIM), SSD_CONV ** -0.5),
        'ssd_conv_b': nrm((N_SSD, SSD_CONV_DIM), 0.01),
        'ssd_dt_bias': dt0 + jnp.log(-jnp.expm1(-dt0)),
        'ssd_a_log': jnp.log(jax.random.uniform(next(keys), (N_SSD, SSD_HEADS), f32, 1.0, 16.0)),
        'ssd_d': gain((N_SSD, SSD_HEADS)),
        'ssd_norm_w': gain((N_SSD, D_INNER)),
        'ssd_w_out': nrm((N_SSD, D_INNER, D_MODEL), D_INNER ** -0.5),
        'sc_w_in': nrm((N_SC, D_MODEL, 3 * D_MODEL), D_MODEL ** -0.5),
        'sc_conv_w': nrm((N_SC, SC_WIDTH, D_MODEL), SC_WIDTH ** -0.5),
        'sc_w_out': nrm((N_SC, D_MODEL, D_MODEL), D_MODEL ** -0.5),
        'cf_w_pw1': nrm((N_CF, D_MODEL, 2 * CF_DIM), D_MODEL ** -0.5),
        'cf_b_pw1': nrm((N_CF, 2 * CF_DIM), 0.01),
        'cf_dw_w': nrm((N_CF, CF_WIDTH, CF_DIM), CF_WIDTH ** -0.5),
        'cf_dw_b': nrm((N_CF, CF_DIM), 0.01),
        'cf_ln_g': gain((N_CF, CF_DIM)),
        'cf_ln_b': nrm((N_CF, CF_DIM), 0.01),
        'cf_w_pw2': nrm((N_CF, CF_DIM, D_MODEL), CF_DIM ** -0.5),
        'cf_b_pw2': nrm((N_CF, D_MODEL), 0.01),
        'ffn_w_gate': nrm((DEPTH, D_MODEL, D_FF), D_MODEL ** -0.5),
        'ffn_w_up': nrm((DEPTH, D_MODEL, D_FF), D_MODEL ** -0.5),
        'ffn_w_down': nrm((DEPTH, D_FF, D_MODEL), D_FF ** -0.5),
    }
    return inp


def reference(x_prompt, x_sample, state_ssm, state_ssm_conv, state_sconv, state_cconv, meta_tokens,
              norm_mix, norm_ffn, norm_final, ssd_w_in, ssd_conv_w, ssd_conv_b, ssd_dt_bias, ssd_a_log,
              ssd_d, ssd_norm_w, ssd_w_out, sc_w_in, sc_conv_w, sc_w_out, cf_w_pw1, cf_b_pw1, cf_dw_w,
              cf_dw_b, cf_ln_g, cf_ln_b, cf_w_pw2, cf_b_pw2, ffn_w_gate, ffn_w_up, ffn_w_down):
    p = dict(norm_mix=norm_mix, norm_ffn=norm_ffn, norm_final=norm_final,
             ssd_w_in=ssd_w_in, ssd_conv_w=ssd_conv_w, ssd_conv_b=ssd_conv_b, ssd_dt_bias=ssd_dt_bias,
             ssd_a_log=ssd_a_log, ssd_d=ssd_d, ssd_norm_w=ssd_norm_w, ssd_w_out=ssd_w_out,
             sc_w_in=sc_w_in, sc_conv_w=sc_conv_w, sc_w_out=sc_w_out,
             cf_w_pw1=cf_w_pw1, cf_b_pw1=cf_b_pw1, cf_dw_w=cf_dw_w, cf_dw_b=cf_dw_b, cf_ln_g=cf_ln_g,
             cf_ln_b=cf_ln_b, cf_w_pw2=cf_w_pw2, cf_b_pw2=cf_b_pw2,
             ffn_w_gate=ffn_w_gate, ffn_w_up=ffn_w_up, ffn_w_down=ffn_w_down)
    dt = x_prompt.dtype
    B = x_prompt.shape[0]
    meta = jnp.broadcast_to(meta_tokens.astype(dt)[None], (B, N_META, D_MODEL))
    xp = jnp.concatenate([meta, x_prompt], axis=1)
    z_ssm = jnp.zeros((N_SSD, B, SSD_HEADS, SSD_HEAD_DIM, SSD_D_STATE), dt)
    z_ssm_conv = jnp.zeros((N_SSD, B, SSD_CONV - 1, SSD_CONV_DIM), dt)
    z_sconv = jnp.zeros((N_SC, B, SC_WIDTH - 1, D_MODEL), dt)
    z_cconv = jnp.zeros((N_CF, B, CF_WIDTH - 1, CF_DIM), dt)
    seq_len = x_prompt.shape[1]
    prompt_segments = ((N_META, N_META), (seq_len, math.gcd(seq_len, SSD_CHUNK)))
    yp, p_ssm, p_ssm_conv, p_sconv, p_cconv = trunk(xp, z_ssm, z_ssm_conv, z_sconv, z_cconv, prompt_segments, p)
    y_prompt = yp[:, N_META:]
    dec_len = x_sample.shape[1]
    sample_segments = ((dec_len, math.gcd(dec_len, SSD_CHUNK)),)
    y_sample, s_ssm, s_ssm_conv, s_sconv, s_cconv = trunk(x_sample, state_ssm, state_ssm_conv, state_sconv,
                                                          state_cconv, sample_segments, p)
    return (y_prompt, y_sample, p_ssm, p_ssm_conv, p_sconv, p_cconv, s_ssm, s_ssm_conv, s_sconv, s_cconv)
```

```python
import functools

import jax
import jax.numpy as jnp
from jax.experimental import pallas as pl
from jax.experimental.pallas import tpu as pltpu

F32 = jnp.float32
BF16 = jnp.bfloat16

EPS = 1e-6
LN_EPS = 1e-5
N_META = 16
SSD_HEAD_DIM = 64
SSD_D_STATE = 128
SSD_GROUPS = 8
SSD_CHUNK = 256
HEADS_PER_GROUP = 4
GROUP_WIDTH = HEADS_PER_GROUP * SSD_HEAD_DIM
LANES = 128
SUBLANES = 8
CF_CARRY_ROWS = 32
VMEM_LIMIT_BYTES = 56 * 1024 * 1024


def _cparams(*sem):
    return pltpu.CompilerParams(dimension_semantics=sem, vmem_limit_bytes=VMEM_LIMIT_BYTES)


def _sigmoid(x):
    return 1.0 / (1.0 + jnp.exp(-x))


def _silu(x):
    return x * _sigmoid(x)


def _softplus(x):
    return jnp.maximum(x, 0.0) + jnp.log1p(jnp.exp(-jnp.abs(x)))


def _rms_rows(x, g):
    return x * jax.lax.rsqrt(jnp.mean(x * x, axis=-1, keepdims=True) + EPS) * g


def _resident(shape):
    return pl.BlockSpec(shape, lambda *_: (0,) * len(shape), pipeline_mode=pl.Buffered(1))


def _norm_matmul_kernel(x_ref, g_ref, w_ref, b_ref, o_ref, hn_ref):
    @pl.when(pl.program_id(1) == 0)
    def _():
        hn_ref[...] = _rms_rows(x_ref[...], g_ref[...]).astype(BF16)

    o_ref[...] = jnp.dot(hn_ref[...], w_ref[...], preferred_element_type=F32) + b_ref[...]


def norm_matmul(x, g, w, b, tm, tn):
    m, d = x.shape
    n = w.shape[1]
    return pl.pallas_call(
        _norm_matmul_kernel,
        grid=(m // tm, n // tn),
        in_specs=[
            pl.BlockSpec((tm, d), lambda i, j: (i, 0)),
            pl.BlockSpec((1, d), lambda i, j: (0, 0)),
            pl.BlockSpec((d, tn), lambda i, j: (0, j)),
            pl.BlockSpec((1, tn), lambda i, j: (0, j)),
        ],
        out_specs=pl.BlockSpec((tm, tn), lambda i, j: (i, j)),
        out_shape=jax.ShapeDtypeStruct((m, n), F32),
        scratch_shapes=[pltpu.VMEM((tm, d), BF16)],
        compiler_params=_cparams("parallel", "arbitrary"),
        name="norm_matmul",
    )(x, g, w, b)


def _matmul_res_kernel(h_ref, w_ref, b_ref, x_ref, o_ref):
    o_ref[...] = x_ref[...] + b_ref[...] + jnp.dot(h_ref[...], w_ref[...], preferred_element_type=F32)


def matmul_res(h, w, b, x, tm):
    m, k = h.shape
    d = w.shape[1]
    return pl.pallas_call(
        _matmul_res_kernel,
        grid=(m // tm,),
        in_specs=[
            pl.BlockSpec((tm, k), lambda i: (i, 0)),
            _resident((k, d)),
            _resident((1, d)),
            pl.BlockSpec((tm, d), lambda i: (i, 0)),
        ],
        out_specs=pl.BlockSpec((tm, d), lambda i: (i, 0)),
        out_shape=jax.ShapeDtypeStruct((m, d), F32),
        compiler_params=_cparams("parallel"),
        name="matmul_res",
    )(h, w, b, x)


def _ffn_kernel(x_ref, g_ref, wg_ref, wu_ref, wd_ref, gf_ref, o_ref, acc_ref, *, ff_chunks, final_norm):
    x = x_ref[...]
    hn = _rms_rows(x, g_ref[...]).astype(BF16)
    acc_ref[...] = x
    for lo, hi in ff_chunks:
        gate = jnp.dot(hn, wg_ref[:, lo:hi], preferred_element_type=F32)
        up = jnp.dot(hn, wu_ref[:, lo:hi], preferred_element_type=F32)
        act = (_silu(gate) * up).astype(BF16)
        acc_ref[...] += jnp.dot(act, wd_ref[lo:hi, :], preferred_element_type=F32)
    y = acc_ref[...]
    if final_norm:
        y = _rms_rows(y, gf_ref[...])
    o_ref[...] = y


def ffn(x, g, wg, wu, wd, gf, tm, final_norm):
    m, d = x.shape
    dff = wg.shape[1]
    step = 2 * 256
    ff_chunks = tuple((lo, min(lo + step, dff)) for lo in range(0, dff, step))
    return pl.pallas_call(
        functools.partial(_ffn_kernel, ff_chunks=ff_chunks, final_norm=final_norm),
        grid=(m // tm,),
        in_specs=[
            pl.BlockSpec((tm, d), lambda i: (i, 0)),
            _resident((1, d)),
            _resident((d, dff)),
            _resident((d, dff)),
            _resident((dff, d)),
            _resident((1, d)),
        ],
        out_specs=pl.BlockSpec((tm, d), lambda i: (i, 0)),
        out_shape=jax.ShapeDtypeStruct((m, d), F32),
        scratch_shapes=[pltpu.VMEM((tm, d), F32)],
        compiler_params=_cparams("parallel"),
        name="ffn",
    )(x, g, wg, wu, wd, gf)


def _cumsum_rows(x):
    rows = x.shape[0]
    row = jax.lax.broadcasted_iota(jnp.int32, x.shape, 0)
    shift = 1
    while shift < rows:
        x = x + jnp.where(row >= shift, pltpu.roll(x, shift, axis=0), 0.0)
        shift *= 2
    return x


def _expand_heads4(cols, lane):
    out = cols[:, 3:4]
    for j in (2, 1, 0):
        out = jnp.where(lane < (j + 1) * SSD_HEAD_DIM, cols[:, j:j + 1], out)
    return out


def _ssd_seq_kernel(z_ref, xa_ref, xb_ref, dt_ref, cw_ref, cb_ref, dtb_ref, alog_ref, dexp_ref, nw_ref,
                    s0_ref, c0_ref, y_ref, sn_ref, cn_ref, st_ref, xpad_ref, xbc_ref, *, rows, n_pad):
    c = pl.program_id(1)
    d_inner = z_ref.shape[1]
    n = SSD_D_STATE
    gw = GROUP_WIDTH

    @pl.when(c == 0)
    def _():
        for g in range(SSD_GROUPS):
            st_ref[g] = s0_ref[0, g * gw:(g + 1) * gw, :].T
        xpad_ref[0:SUBLANES, :] = c0_ref[0]

    xpad_ref[SUBLANES:SUBLANES + rows, 0:d_inner] = xa_ref[...]
    xpad_ref[SUBLANES:SUBLANES + rows, d_inner:2 * d_inner] = xb_ref[...]
    for s in range(2 * d_inner // gw):
        sl = slice(s * gw, (s + 1) * gw)
        acc = cb_ref[:, sl] + cw_ref[3:4, sl] * xpad_ref[SUBLANES:SUBLANES + rows, sl]
        for k in range(3):
            acc = acc + cw_ref[k:k + 1, sl] * xpad_ref[SUBLANES - 3 + k:SUBLANES - 3 + k + rows, sl]
        xbc_ref[:, sl] = _silu(acc)
    carry = xpad_ref[rows:rows + SUBLANES, :]
    xpad_ref[0:SUBLANES, :] = carry

    dt = _softplus(dt_ref[...] + dtb_ref[...])
    if n_pad:
        prow = jax.lax.broadcasted_iota(jnp.int32, dt.shape, 0)
        dt = jnp.where(prow >= n_pad, dt, 0.0)
    a = -jnp.exp(alog_ref[...])
    cs = _cumsum_rows(dt * a)
    cs_t = cs.T
    dt_t = dt.T
    cs_last = cs[rows - 1:rows, :]
    e_cs = jnp.exp(cs)
    w_end = dt * jnp.exp(cs_last - cs)
    e_last = jnp.exp(cs_last)

    ti = jax.lax.broadcasted_iota(jnp.int32, (rows, rows), 0)
    si = jax.lax.broadcasted_iota(jnp.int32, (rows, rows), 1)
    causal = ti >= si
    lane_g = jax.lax.broadcasted_iota(jnp.int32, (rows, gw), 1)
    lane_1 = jax.lax.broadcasted_iota(jnp.int32, (1, gw), 1)

    for g in range(SSD_GROUPS):
        gsl = slice(g * gw, (g + 1) * gw)
        hsl = slice(g * HEADS_PER_GROUP, (g + 1) * HEADS_PER_GROUP)
        xg = xbc_ref[:, gsl]
        xg_b = xg.astype(BF16)
        bg_t = xbc_ref[:, d_inner + g * n:d_inner + (g + 1) * n].T.astype(BF16)
        cg_b = xbc_ref[:, d_inner + SSD_GROUPS * n + g * n:d_inner + SSD_GROUPS * n + (g + 1) * n].astype(BF16)
        cb = jnp.dot(cg_b, bg_t, preferred_element_type=F32)
        st = st_ref[g]
        y = jnp.dot(cg_b, st.astype(BF16), preferred_element_type=F32) * _expand_heads4(e_cs[:, hsl], lane_g)
        for j in range(HEADS_PER_GROUP):
            h = g * HEADS_PER_GROUP + j
            seg = cs[:, h:h + 1] - cs_t[h:h + 1, :]
            m = cb * jnp.exp(jnp.where(causal, seg, -jnp.inf)) * dt_t[h:h + 1, :]
            in_head = (lane_g >= j * SSD_HEAD_DIM) & (lane_g < (j + 1) * SSD_HEAD_DIM)
            xm = jnp.where(in_head, xg_b, jnp.zeros_like(xg_b))
            y = y + jnp.dot(m.astype(BF16), xm, preferred_element_type=F32)
        y = y + dexp_ref[:, gsl] * xg
        wg = (xg * _expand_heads4(w_end[:, hsl], lane_g)).astype(BF16)
        st_ref[g] = st * _expand_heads4(e_last[:, hsl], lane_1) + jnp.dot(bg_t, wg, preferred_element_type=F32)
        y = y * _silu(z_ref[:, gsl])
        y_ref[:, gsl] = _rms_rows(y, nw_ref[:, gsl]).astype(BF16)

    @pl.when(c == pl.num_programs(1) - 1)
    def _():
        for g in range(SSD_GROUPS):
            sn_ref[0, g * gw:(g + 1) * gw, :] = st_ref[g].T
        cn_ref[0] = carry


def ssd_seq(zx, conv_w, conv_b, dt_bias, a_log, d_exp, norm_w, s0, c0, nb, seq, n_pad):
    rows = SSD_CHUNK
    d_inner = norm_w.shape[1]
    conv_dim = conv_w.shape[1]
    hp = s0.shape[1]
    nc = seq // rows
    dt_blk = (d_inner + conv_dim) // LANES
    shared0 = s0.shape[0] == 1

    def row_blk(width, col):
        return pl.BlockSpec((rows, width), lambda b, c: (b * nc + c, col))

    def init_blk(shape):
        return pl.BlockSpec((1,) + shape, lambda b, c: (0 if shared0 else b, 0, 0))

    def const(shape):
        return pl.BlockSpec(shape, lambda b, c: (0, 0))

    return pl.pallas_call(
        functools.partial(_ssd_seq_kernel, rows=rows, n_pad=n_pad),
        grid=(nb, nc),
        in_specs=[
            row_blk(d_inner, 0), row_blk(d_inner, 1), row_blk(d_inner, 2), row_blk(LANES, dt_blk),
            const((4, conv_dim)), const((1, conv_dim)), const((1, LANES)), const((1, LANES)),
            const((1, d_inner)), const((1, d_inner)),
            init_blk((hp, SSD_D_STATE)), init_blk((SUBLANES, conv_dim)),
        ],
        out_specs=[
            pl.BlockSpec((rows, d_inner), lambda b, c: (b * nc + c, 0)),
            pl.BlockSpec((1, hp, SSD_D_STATE), lambda b, c: (b, 0, 0)),
            pl.BlockSpec((1, SUBLANES, conv_dim), lambda b, c: (b, 0, 0)),
        ],
        out_shape=[
            jax.ShapeDtypeStruct((nb * seq, d_inner), BF16),
            jax.ShapeDtypeStruct((nb, hp, SSD_D_STATE), F32),
            jax.ShapeDtypeStruct((nb, SUBLANES, conv_dim), F32),
        ],
        scratch_shapes=[
            pltpu.VMEM((SSD_GROUPS, SSD_D_STATE, GROUP_WIDTH), F32),
            pltpu.VMEM((rows + SUBLANES, conv_dim), F32),
            pltpu.VMEM((rows, conv_dim), F32),
        ],
        compiler_params=_cparams("parallel", "arbitrary"),
        name="ssd_seq",
    )(zx, zx, zx, zx, conv_w, conv_b, dt_bias, a_log, d_exp, norm_w, s0, c0)


def _sc_seq_kernel(bg_ref, cg_ref, xv_ref, w_ref, c0_ref, h_ref, cn_ref, ppad_ref, *, rows):
    r = pl.program_id(1)

    @pl.when(r == 0)
    def _():
        ppad_ref[0:SUBLANES, :] = c0_ref[0]

    p = cg_ref[...] * xv_ref[...]
    ppad_ref[SUBLANES:SUBLANES + rows, :] = p
    v = (w_ref[0:1, :] * ppad_ref[SUBLANES - 2:SUBLANES - 2 + rows, :]
         + w_ref[1:2, :] * ppad_ref[SUBLANES - 1:SUBLANES - 1 + rows, :]
         + w_ref[2:3, :] * p)
    h_ref[...] = (bg_ref[...] * v).astype(BF16)
    carry = ppad_ref[rows:rows + SUBLANES, :]
    ppad_ref[0:SUBLANES, :] = carry

    @pl.when(r == pl.num_programs(1) - 1)
    def _():
        cn_ref[0] = carry


def sc_seq(bcx, conv_w, c0, nb, seq, tm):
    d = conv_w.shape[1]
    nr = seq // tm
    shared0 = c0.shape[0] == 1

    def row_blk(col):
        return pl.BlockSpec((tm, d), lambda b, r: (b * nr + r, col))

    return pl.pallas_call(
        functools.partial(_sc_seq_kernel, rows=tm),
        grid=(nb, nr),
        in_specs=[
            row_blk(0), row_blk(1), row_blk(2),
            pl.BlockSpec((3, d), lambda b, r: (0, 0)),
            pl.BlockSpec((1, SUBLANES, d), lambda b, r: (0 if shared0 else b, 0, 0)),
        ],
        out_specs=[
            pl.BlockSpec((tm, d), lambda b, r: (b * nr + r, 0)),
            pl.BlockSpec((1, SUBLANES, d), lambda b, r: (b, 0, 0)),
        ],
        out_shape=[
            jax.ShapeDtypeStruct((nb * seq, d), BF16),
            jax.ShapeDtypeStruct((nb, SUBLANES, d), F32),
        ],
        scratch_shapes=[pltpu.VMEM((tm + SUBLANES, d), F32)],
        compiler_params=_cparams("parallel", "arbitrary"),
        name="sc_seq",
    )(bcx, bcx, bcx, conv_w, c0)


def _cf_seq_kernel(a1_ref, a2_ref, w_ref, b_ref, lg_ref, lb_ref, c0_ref, h_ref, cn_ref, upad_ref, v_ref,
                   *, rows, taps):
    r = pl.program_id(1)
    hist = CF_CARRY_ROWS

    @pl.when(r == 0)
    def _():
        upad_ref[0:hist, :] = c0_ref[0]

    u = a1_ref[...] * _sigmoid(a2_ref[...])
    upad_ref[hist:hist + rows, :] = u
    d = u.shape[1]
    for s in range(d // LANES):
        sl = slice(s * LANES, (s + 1) * LANES)
        acc = b_ref[:, sl] + w_ref[taps - 1:taps, sl] * upad_ref[hist:hist + rows, sl]
        for k in range(taps - 1):
            off = hist - (taps - 1) + k
            acc = acc + w_ref[k:k + 1, sl] * upad_ref[off:off + rows, sl]
        v_ref[:, sl] = acc
    v = v_ref[...]
    mu = jnp.mean(v, axis=-1, keepdims=True)
    dv = v - mu
    var = jnp.mean(dv * dv, axis=-1, keepdims=True)
    y = dv * jax.lax.rsqrt(var + LN_EPS) * lg_ref[...] + lb_ref[...]
    h_ref[...] = _silu(y).astype(BF16)
    carry = upad_ref[rows:rows + hist, :]
    upad_ref[0:hist, :] = carry

    @pl.when(r == pl.num_programs(1) - 1)
    def _():
        cn_ref[0] = carry


def cf_seq(a, dw_w, dw_b, ln_g, ln_b, c0, nb, seq, tm):
    taps, d = dw_w.shape
    nr = seq // tm
    shared0 = c0.shape[0] == 1

    def row_blk(col):
        return pl.BlockSpec((tm, d), lambda b, r: (b * nr + r, col))

    def const(shape):
        return pl.BlockSpec(shape, lambda b, r: (0, 0))

    return pl.pallas_call(
        functools.partial(_cf_seq_kernel, rows=tm, taps=taps),
        grid=(nb, nr),
        in_specs=[
            row_blk(0), row_blk(1),
            const((taps, d)), const((1, d)), const((1, d)), const((1, d)),
            pl.BlockSpec((1, CF_CARRY_ROWS, d), lambda b, r: (0 if shared0 else b, 0, 0)),
        ],
        out_specs=[
            pl.BlockSpec((tm, d), lambda b, r: (b * nr + r, 0)),
            pl.BlockSpec((1, CF_CARRY_ROWS, d), lambda b, r: (b, 0, 0)),
        ],
        out_shape=[
            jax.ShapeDtypeStruct((nb * seq, d), BF16),
            jax.ShapeDtypeStruct((nb, CF_CARRY_ROWS, d), F32),
        ],
        scratch_shapes=[pltpu.VMEM((tm + CF_CARRY_ROWS, d), F32), pltpu.VMEM((tm, d), F32)],
        compiler_params=_cparams("parallel", "arbitrary"),
        name="cf_seq",
    )(a, a, dw_w, dw_b, ln_g, ln_b, c0)


def _expand_heads_mxu(v, d_inner):
    hrow = jax.lax.broadcasted_iota(jnp.int32, (LANES, d_inner), 0)
    col = jax.lax.broadcasted_iota(jnp.int32, (LANES, d_inner), 1)
    onehot = jnp.where((col >= hrow * SSD_HEAD_DIM) & (col < (hrow + 1) * SSD_HEAD_DIM), 1.0, 0.0).astype(BF16)
    hi = v.astype(BF16)
    r1 = v - hi.astype(F32)
    mid = r1.astype(BF16)
    lo = (r1 - mid.astype(F32)).astype(BF16)
    out = jnp.dot(hi, onehot, preferred_element_type=F32)
    out = out + jnp.dot(mid, onehot, preferred_element_type=F32)
    return out + jnp.dot(lo, onehot, preferred_element_type=F32)


def _ssd_dec_pre_kernel(xa_ref, xb_ref, dt_ref, cbuf_ref, cw_ref, cb_ref, dtb_ref, alog_ref,
                        xs_ref, xdt_ref, b_ref, c_ref, da_ref, cnew_ref):
    conv_dim = cw_ref.shape[1]
    d_inner = xs_ref.shape[1]
    gn = b_ref.shape[1]

    def conv_silu(xnew, lo, hi):
        acc = cb_ref[:, lo:hi] + cw_ref[3:4, lo:hi] * xnew
        for k in range(3):
            acc = acc + cw_ref[k:k + 1, lo:hi] * cbuf_ref[:, k * conv_dim + lo:k * conv_dim + hi]
        return _silu(acc)

    xa = xa_ref[...]
    xb = xb_ref[...]
    cnew_ref[:, 0:2 * conv_dim] = cbuf_ref[:, conv_dim:3 * conv_dim]
    cnew_ref[:, 2 * conv_dim:2 * conv_dim + d_inner] = xa
    cnew_ref[:, 2 * conv_dim + d_inner:3 * conv_dim] = xb
    dt = _softplus(dt_ref[...] + dtb_ref[...])
    da_ref[...] = jnp.exp(dt * -jnp.exp(alog_ref[...]))
    xs = conv_silu(xa, 0, d_inner)
    xs_ref[...] = xs
    xdt_ref[...] = xs * _expand_heads_mxu(dt, d_inner)
    bc = conv_silu(xb, d_inner, conv_dim)
    b_ref[...] = bc[:, 0:gn]
    c_ref[...] = bc[:, gn:2 * gn]


def ssd_dec_pre(zx, cbuf, conv_w, conv_b, dt_bias, a_log, d_inner):
    nb = zx.shape[0]
    conv_dim = conv_w.shape[1]
    assert conv_dim == 2 * d_inner
    gn = (conv_dim - d_inner) // 2
    dt_blk = (d_inner + conv_dim) // LANES

    def full(shape):
        return pl.BlockSpec(shape, lambda i: (0, 0))

    return pl.pallas_call(
        _ssd_dec_pre_kernel,
        grid=(1,),
        in_specs=[
            pl.BlockSpec((nb, d_inner), lambda i: (0, 1)),
            pl.BlockSpec((nb, d_inner), lambda i: (0, 2)),
            pl.BlockSpec((nb, LANES), lambda i: (0, dt_blk)),
            full((nb, 3 * conv_dim)), full((4, conv_dim)), full((1, conv_dim)), full((1, LANES)), full((1, LANES)),
        ],
        out_specs=[
            full((nb, d_inner)), full((nb, d_inner)), full((nb, gn)), full((nb, gn)), full((nb, LANES)),
            full((nb, 3 * conv_dim)),
        ],
        out_shape=[
            jax.ShapeDtypeStruct((nb, d_inner), F32), jax.ShapeDtypeStruct((nb, d_inner), F32),
            jax.ShapeDtypeStruct((nb, gn), F32), jax.ShapeDtypeStruct((nb, gn), F32),
            jax.ShapeDtypeStruct((nb, LANES), F32), jax.ShapeDtypeStruct((nb, 3 * conv_dim), F32),
        ],
        compiler_params=_cparams("arbitrary"),
        name="ssd_dec_pre",
    )(zx, zx, zx, cbuf, conv_w, conv_b, dt_bias, a_log)


def _ssd_dec_state_kernel(s_ref, xdt_t_ref, b_ref, c_ref, da_ref, sn_ref, y_t_ref):
    b = pl.program_id(0)
    hp, nb = xdt_t_ref.shape
    n = SSD_D_STATE
    gw = GROUP_WIDTH

    @pl.when(b == 0)
    def _():
        y_t_ref[...] = jnp.zeros_like(y_t_ref)

    da_row = da_ref[pl.ds(b, 1), :]
    b_row = b_ref[pl.ds(b, 1), :]
    c_row = c_ref[pl.ds(b, 1), :]
    is_b = jax.lax.broadcasted_iota(jnp.int32, (SSD_HEAD_DIM, nb), 1) == b
    for g in range(hp // gw):
        bg = b_row[:, g * n:(g + 1) * n]
        cg = c_row[:, g * n:(g + 1) * n]
        for j in range(HEADS_PER_GROUP):
            h = g * HEADS_PER_GROUP + j
            hsl = slice(g * gw + j * SSD_HEAD_DIM, g * gw + (j + 1) * SSD_HEAD_DIM)
            xcol = jnp.sum(jnp.where(is_b, xdt_t_ref[hsl, :], 0.0), axis=1, keepdims=True)
            snew = s_ref[0, hsl, :] * da_row[:, h:h + 1] + xcol * bg
            sn_ref[0, hsl, :] = snew
            ycol = jnp.sum(snew * cg, axis=1, keepdims=True)
            y_t_ref[hsl, :] = jnp.where(is_b, ycol, y_t_ref[hsl, :])


def ssd_dec_state(state, xdt_t, bm, cm, da):
    nb, hp, n = state.shape

    def full(shape):
        return pl.BlockSpec(shape, lambda b: (0, 0))

    return pl.pallas_call(
        _ssd_dec_state_kernel,
        grid=(nb,),
        in_specs=[
            pl.BlockSpec((1, hp, n), lambda b: (b, 0, 0)),
            full((hp, nb)), full(bm.shape), full(cm.shape), full(da.shape),
        ],
        out_specs=[pl.BlockSpec((1, hp, n), lambda b: (b, 0, 0)), full((hp, nb))],
        out_shape=[jax.ShapeDtypeStruct((nb, hp, n), F32), jax.ShapeDtypeStruct((hp, nb), F32)],
        compiler_params=_cparams("arbitrary"),
        name="ssd_dec_state",
    )(state, xdt_t, bm, cm, da)


def _ssd_dec_post_kernel(y_ref, xs_ref, z_ref, dexp_ref, nw_ref, h_ref):
    gw = GROUP_WIDTH
    y = (y_ref[...] + dexp_ref[...] * xs_ref[...]) * _silu(z_ref[...])
    for g in range(y.shape[1] // gw):
        gsl = slice(g * gw, (g + 1) * gw)
        h_ref[:, gsl] = _rms_rows(y[:, gsl], nw_ref[:, gsl]).astype(BF16)


def ssd_dec_post(y, xs, zx, d_exp, norm_w):
    nb, d_inner = y.shape

    def full(shape):
        return pl.BlockSpec(shape, lambda i: (0, 0))

    return pl.pallas_call(
        _ssd_dec_post_kernel,
        grid=(1,),
        in_specs=[full((nb, d_inner)), full((nb, d_inner)), full((nb, d_inner)), full((1, d_inner)), full((1, d_inner))],
        out_specs=full((nb, d_inner)),
        out_shape=jax.ShapeDtypeStruct((nb, d_inner), BF16),
        compiler_params=_cparams("arbitrary"),
        name="ssd_dec_post",
    )(y, xs, zx, d_exp, norm_w)


def _sc_dec_kernel(bg_ref, cg_ref, xv_ref, w_ref, buf_ref, h_ref, bn_ref):
    d = w_ref.shape[1]
    p = cg_ref[...] * xv_ref[...]
    v = w_ref[0:1, :] * buf_ref[:, 0:d] + w_ref[1:2, :] * buf_ref[:, d:2 * d] + w_ref[2:3, :] * p
    h_ref[...] = (bg_ref[...] * v).astype(BF16)
    bn_ref[:, 0:d] = buf_ref[:, d:2 * d]
    bn_ref[:, d:2 * d] = p


def sc_dec(bcx, conv_w, buf):
    nb = bcx.shape[0]
    d = conv_w.shape[1]

    def full(shape, col=0):
        return pl.BlockSpec(shape, lambda i: (0, col))

    return pl.pallas_call(
        _sc_dec_kernel,
        grid=(1,),
        in_specs=[full((nb, d), 0), full((nb, d), 1), full((nb, d), 2), full((3, d)), full((nb, 2 * d))],
        out_specs=[full((nb, d)), full((nb, 2 * d))],
        out_shape=[jax.ShapeDtypeStruct((nb, d), BF16), jax.ShapeDtypeStruct((nb, 2 * d), F32)],
        compiler_params=_cparams("arbitrary"),
        name="sc_dec",
    )(bcx, bcx, bcx, conv_w, buf)


def _cf_dec_kernel(a1_ref, a2_ref, w_ref, b_ref, lg_ref, lb_ref, buf_ref, h_ref, bn_ref):
    taps, d = w_ref.shape
    u = a1_ref[...] * _sigmoid(a2_ref[...])
    v = b_ref[...] + w_ref[taps - 1:taps, :] * u
    for k in range(taps - 1):
        v = v + w_ref[k:k + 1, :] * buf_ref[:, k * d:(k + 1) * d]
    mu = jnp.mean(v, axis=-1, keepdims=True)
    dv = v - mu
    var = jnp.mean(dv * dv, axis=-1, keepdims=True)
    y = dv * jax.lax.rsqrt(var + LN_EPS) * lg_ref[...] + lb_ref[...]
    h_ref[...] = _silu(y).astype(BF16)
    bn_ref[:, 0:(taps - 2) * d] = buf_ref[:, d:(taps - 1) * d]
    bn_ref[:, (taps - 2) * d:(taps - 1) * d] = u


def cf_dec(a, dw_w, dw_b, ln_g, ln_b, buf):
    nb = a.shape[0]
    taps, d = dw_w.shape

    def full(shape, col=0):
        return pl.BlockSpec(shape, lambda i: (0, col))

    return pl.pallas_call(
        _cf_dec_kernel,
        grid=(1,),
        in_specs=[full((nb, d), 0), full((nb, d), 1), full((taps, d)), full((1, d)), full((1, d)), full((1, d)),
                  full((nb, (taps - 1) * d))],
        out_specs=[full((nb, d)), full((nb, (taps - 1) * d))],
        out_shape=[jax.ShapeDtypeStruct((nb, d), BF16), jax.ShapeDtypeStruct((nb, (taps - 1) * d), F32)],
        compiler_params=_cparams("arbitrary"),
        name="cf_dec",
    )(a, a, dw_w, dw_b, ln_g, ln_b, buf)


def _pad_cols(w, n):
    return jnp.pad(w, ((0, 0), (0, n - w.shape[1])))


def _tail_tile(rows_valid, tile_rows, d):
    del rows_valid
    return jnp.zeros((1, tile_rows, d), F32)


def kernel(x_prompt, x_sample, state_ssm, state_ssm_conv, state_sconv, state_cconv, meta_tokens,
           norm_mix, norm_ffn, norm_final, ssd_w_in, ssd_conv_w, ssd_conv_b, ssd_dt_bias, ssd_a_log,
           ssd_d, ssd_norm_w, ssd_w_out, sc_w_in, sc_conv_w, sc_w_out, cf_w_pw1, cf_b_pw1, cf_dw_w,
           cf_dw_b, cf_ln_g, cf_ln_b, cf_w_pw2, cf_b_pw2, ffn_w_gate, ffn_w_up, ffn_w_down):
    nb, seq, d = x_prompt.shape
    nd = x_sample.shape[0]
    assert x_sample.shape[1] == 1
    depth = norm_mix.shape[0]
    n_meta = meta_tokens.shape[0]
    heads = ssd_a_log.shape[1]
    d_inner = ssd_norm_w.shape[1]
    conv_dim = ssd_conv_w.shape[2]
    hp = heads * SSD_HEAD_DIM
    assert hp == d_inner and heads <= LANES and seq % SSD_CHUNK == 0
    ssd_n = -(-(d_inner + conv_dim + LANES) // 1280) * 1280
    tm_main = 1024
    ns = nd + n_meta

    x_main = x_prompt.reshape(nb * seq, d)
    x_small = jnp.concatenate([x_sample.reshape(nd, d), meta_tokens.astype(F32)], axis=0)
    zeros_d = jnp.zeros((1, d), F32)

    def row(v):
        return v.reshape(1, -1).astype(F32)

    def pad_lanes(v):
        return jnp.pad(v.reshape(1, -1).astype(F32), ((0, 0), (0, LANES - v.shape[-1])))

    p_ssm, p_ssm_conv, p_sconv, p_cconv = [], [], [], []
    s_ssm, s_ssm_conv, s_sconv, s_cconv = [], [], [], []

    for i in range(depth):
        kind, j = i % 3, i // 3
        g_mix = row(norm_mix[i])
        if kind == 0:
            w_in = _pad_cols(ssd_w_in[j], ssd_n).astype(BF16)
            b_in = jnp.zeros((1, ssd_n), F32)
            conv_w, conv_b = ssd_conv_w[j], row(ssd_conv_b[j])
            dt_bias, a_log = pad_lanes(ssd_dt_bias[j]), pad_lanes(ssd_a_log[j])
            d_exp = jnp.repeat(ssd_d[j].astype(F32), SSD_HEAD_DIM).reshape(1, d_inner)
            norm_w = row(ssd_norm_w[j])
            w_out, b_out = ssd_w_out[j].astype(BF16), zeros_d
            zx_s = norm_matmul(x_small, g_mix, w_in, b_in, ns, 1280)
            zx_d = zx_s[:nd]
            cbuf = state_ssm_conv[j].reshape(nd, 3 * conv_dim)
            xs, xdt, bm, cm, da, cnew = ssd_dec_pre(zx_d, cbuf, conv_w, conv_b, dt_bias, a_log, d_inner)
            st_new, y_t = ssd_dec_state(state_ssm[j].reshape(nd, hp, SSD_D_STATE), xdt.T, bm, cm, da)
            h_d = ssd_dec_post(y_t.T, xs, zx_d, d_exp, norm_w)
            s_ssm.append(st_new.reshape(state_ssm.shape[1:]))
            s_ssm_conv.append(cnew.reshape(nd, 3, conv_dim))
            zx_m = jnp.pad(zx_s[nd:], ((SSD_CHUNK - n_meta, 0), (0, 0)))
            h_m, st_m, ct_m = ssd_seq(zx_m, conv_w, conv_b, dt_bias, a_log, d_exp, norm_w,
                                      jnp.zeros((1, hp, SSD_D_STATE), F32), jnp.zeros((1, SUBLANES, conv_dim), F32),
                                      1, SSD_CHUNK, SSD_CHUNK - n_meta)
            h_small = jnp.concatenate([h_d, h_m[SSD_CHUNK - n_meta:]], axis=0)
            zx = norm_matmul(x_main, g_mix, w_in, b_in, tm_main, 1280)
            h_main, st_p, ct_p = ssd_seq(zx, conv_w, conv_b, dt_bias, a_log, d_exp, norm_w, st_m, ct_m, nb, seq, 0)
            p_ssm.append(st_p.reshape((nb,) + state_ssm.shape[2:]))
            p_ssm_conv.append(ct_p[:, SUBLANES - 3:, :])
        elif kind == 1:
            w_in, b_in = sc_w_in[j].astype(BF16), jnp.zeros((1, 3 * d), F32)
            conv_w = sc_conv_w[j]
            w_out, b_out = sc_w_out[j].astype(BF16), zeros_d
            bcx_s = norm_matmul(x_small, g_mix, w_in, b_in, ns, 1024)
            h_d, bnew = sc_dec(bcx_s[:nd], conv_w, state_sconv[j].reshape(nd, 2 * d))
            s_sconv.append(bnew.reshape(nd, 2, d))
            h_m, ct_m = sc_seq(bcx_s[nd:], conv_w, jnp.zeros((1, SUBLANES, d), F32), 1, n_meta, n_meta)
            h_small = jnp.concatenate([h_d, h_m], axis=0)
            bcx = norm_matmul(x_main, g_mix, w_in, b_in, tm_main, 1024)
            h_main, ct_p = sc_seq(bcx, conv_w, ct_m, nb, seq, 512)
            p_sconv.append(ct_p[:, SUBLANES - 2:, :])
        else:
            w_in, b_in = cf_w_pw1[j].astype(BF16), row(cf_b_pw1[j])
            dw_w, dw_b, ln_g, ln_b = cf_dw_w[j], row(cf_dw_b[j]), row(cf_ln_g[j]), row(cf_ln_b[j])
            taps = dw_w.shape[0]
            w_out, b_out = cf_w_pw2[j].astype(BF16), row(cf_b_pw2[j])
            a_s = norm_matmul(x_small, g_mix, w_in, b_in, ns, 1024)
            h_d, bnew = cf_dec(a_s[:nd], dw_w, dw_b, ln_g, ln_b, state_cconv[j].reshape(nd, (taps - 1) * d))
            s_cconv.append(bnew.reshape(nd, taps - 1, d))
            h_m, ct_m = cf_seq(a_s[nd:], dw_w, dw_b, ln_g, ln_b, jnp.zeros((1, CF_CARRY_ROWS, d), F32), 1, n_meta, n_meta)
            h_small = jnp.concatenate([h_d, h_m], axis=0)
            a = norm_matmul(x_main, g_mix, w_in, b_in, tm_main, 1024)
            h_main, ct_p = cf_seq(a, dw_w, dw_b, ln_g, ln_b, ct_m, nb, seq, 256)
            p_cconv.append(ct_p[:, CF_CARRY_ROWS - (taps - 1):, :])

        g_ffn, g_fin = row(norm_ffn[i]), row(norm_final)
        wg, wu, wd = ffn_w_gate[i].astype(BF16), ffn_w_up[i].astype(BF16), ffn_w_down[i].astype(BF16)
        last = i == depth - 1
        x_small = ffn(matmul_res(h_small, w_out, b_out, x_small, ns), g_ffn, wg, wu, wd, g_fin, ns, last)
        x_main = ffn(matmul_res(h_main, w_out, b_out, x_main, tm_main), g_ffn, wg, wu, wd, g_fin, 512, last)

    y_prompt = x_main.reshape(nb, seq, d)
    y_sample = x_small[:nd].reshape(nd, 1, d)
    return (y_prompt, y_sample, jnp.stack(p_ssm), jnp.stack(p_ssm_conv), jnp.stack(p_sconv), jnp.stack(p_cconv),
            jnp.stack(s_ssm), jnp.stack(s_ssm_conv), jnp.stack(s_sconv), jnp.stack(s_cconv))
```

```python
import functools

import jax
import jax.numpy as jnp
from jax.experimental import pallas as pl
from jax.experimental.pallas import tpu as pltpu

F32 = jnp.float32
BF16 = jnp.bfloat16

EPS = 1e-6
LN_EPS = 1e-5
SSD_HEAD_DIM = 64
SSD_D_STATE = 128
SSD_GROUPS = 8
SSD_CHUNK = 256
SSD_SUBCHUNK = 128
HEADS_PER_GROUP = 4
GROUP_WIDTH = HEADS_PER_GROUP * SSD_HEAD_DIM
LANES = 128
SUBLANES = 8
CF_CARRY_ROWS = 32
DEC_SEQ_BLOCK = 8
VMEM_LIMIT_BYTES = 56 * 1024 * 1024


def _cparams(*sem):
    return pltpu.CompilerParams(dimension_semantics=sem, vmem_limit_bytes=VMEM_LIMIT_BYTES)


def _sigmoid(x):
    return 1.0 / (1.0 + jnp.exp(-x))


def _silu(x):
    return x * _sigmoid(x)


def _softplus(x):
    return jnp.maximum(x, 0.0) + jnp.log1p(jnp.exp(-jnp.abs(x)))


def _rms_rows(x, g):
    return x * jax.lax.rsqrt(jnp.mean(x * x, axis=-1, keepdims=True) + EPS) * g


def _col_tile(n, limit):
    return max(t for t in range(LANES, limit + 1, LANES) if n % t == 0)


def _resident(shape):
    return pl.BlockSpec(shape, lambda *_: (0,) * len(shape), pipeline_mode=pl.Buffered(1))


def _norm_matmul_kernel(x_ref, g_ref, w_ref, b_ref, o_ref, hn_ref):
    @pl.when(pl.program_id(1) == 0)
    def _():
        hn_ref[...] = _rms_rows(x_ref[...], g_ref[...]).astype(BF16)

    o_ref[...] = jnp.dot(hn_ref[...], w_ref[...], preferred_element_type=F32) + b_ref[...]


def norm_matmul(x, g, w, b, tm, tn):
    m, d = x.shape
    n = w.shape[1]
    return pl.pallas_call(
        _norm_matmul_kernel,
        grid=(m // tm, n // tn),
        in_specs=[
            pl.BlockSpec((tm, d), lambda i, j: (i, 0)),
            pl.BlockSpec((1, d), lambda i, j: (0, 0)),
            pl.BlockSpec((d, tn), lambda i, j: (0, j)),
            pl.BlockSpec((1, tn), lambda i, j: (0, j)),
        ],
        out_specs=pl.BlockSpec((tm, tn), lambda i, j: (i, j)),
        out_shape=jax.ShapeDtypeStruct((m, n), F32),
        scratch_shapes=[pltpu.VMEM((tm, d), BF16)],
        compiler_params=_cparams("parallel", "arbitrary"),
        name="norm_matmul",
    )(x, g, w, b)


def _matmul_res_kernel(h_ref, w_ref, b_ref, x_ref, o_ref):
    o_ref[...] = x_ref[...] + b_ref[...] + jnp.dot(h_ref[...], w_ref[...], preferred_element_type=F32)


def matmul_res(h, w, b, x, tm):
    m, k = h.shape
    d = w.shape[1]
    return pl.pallas_call(
        _matmul_res_kernel,
        grid=(m // tm,),
        in_specs=[
            pl.BlockSpec((tm, k), lambda i: (i, 0)),
            _resident((k, d)),
            _resident((1, d)),
            pl.BlockSpec((tm, d), lambda i: (i, 0)),
        ],
        out_specs=pl.BlockSpec((tm, d), lambda i: (i, 0)),
        out_shape=jax.ShapeDtypeStruct((m, d), F32),
        compiler_params=_cparams("parallel"),
        name="matmul_res",
    )(h, w, b, x)


def _ffn_kernel(x_ref, g_ref, wg_ref, wu_ref, wd_ref, gf_ref, o_ref, acc_ref, *, ff_chunks, final_norm):
    x = x_ref[...]
    hn = _rms_rows(x, g_ref[...]).astype(BF16)
    acc_ref[...] = x
    for lo, hi in ff_chunks:
        gate = jnp.dot(hn, wg_ref[:, lo:hi], preferred_element_type=F32)
        up = jnp.dot(hn, wu_ref[:, lo:hi], preferred_element_type=F32)
        act = (_silu(gate) * up).astype(BF16)
        acc_ref[...] += jnp.dot(act, wd_ref[lo:hi, :], preferred_element_type=F32)
    y = acc_ref[...]
    if final_norm:
        y = _rms_rows(y, gf_ref[...])
    o_ref[...] = y


def ffn(x, g, wg, wu, wd, gf, tm, final_norm):
    m, d = x.shape
    dff = wg.shape[1]
    step = 2 * 256
    ff_chunks = tuple((lo, min(lo + step, dff)) for lo in range(0, dff, step))
    return pl.pallas_call(
        functools.partial(_ffn_kernel, ff_chunks=ff_chunks, final_norm=final_norm),
        grid=(m // tm,),
        in_specs=[
            pl.BlockSpec((tm, d), lambda i: (i, 0)),
            _resident((1, d)),
            _resident((d, dff)),
            _resident((d, dff)),
            _resident((dff, d)),
            _resident((1, d)),
        ],
        out_specs=pl.BlockSpec((tm, d), lambda i: (i, 0)),
        out_shape=jax.ShapeDtypeStruct((m, d), F32),
        scratch_shapes=[pltpu.VMEM((tm, d), F32)],
        compiler_params=_cparams("parallel"),
        name="ffn",
    )(x, g, wg, wu, wd, gf)


def _cumsum_rows(x):
    rows = x.shape[0]
    row = jax.lax.broadcasted_iota(jnp.int32, x.shape, 0)
    shift = 1
    while shift < rows:
        x = x + jnp.where(row >= shift, pltpu.roll(x, shift, axis=0), 0.0)
        shift *= 2
    return x


def _expand_heads4(cols, lane):
    out = cols[:, 3:4]
    for j in (2, 1, 0):
        out = jnp.where(lane < (j + 1) * SSD_HEAD_DIM, cols[:, j:j + 1], out)
    return out


def _ssd_seq_kernel(x_ref, gm_ref, win_ref, cw_ref, cb_ref, dtb_ref, alog_ref, dexp_ref, nw_ref, wout_ref,
                    s0_ref, c0_ref, o_ref, sn_ref, cn_ref, st_ref, xpad_ref, xbc_ref, z_ref, y_ref,
                    *, rows, sub, n_pad):
    c = pl.program_id(1)
    d_inner = z_ref.shape[1]
    conv_dim = xbc_ref.shape[1]
    n = SSD_D_STATE
    gw = GROUP_WIDTH
    pw = 2 * gw

    @pl.when(c == 0)
    def _():
        for g in range(SSD_GROUPS):
            st_ref[g] = s0_ref[0, g * gw:(g + 1) * gw, :].T
        xpad_ref[0:SUBLANES, :] = c0_ref[0]

    x = x_ref[...]
    hn = _rms_rows(x, gm_ref[...]).astype(BF16)
    for s in range(d_inner // pw):
        z_ref[:, s * pw:(s + 1) * pw] = jnp.dot(hn, win_ref[:, s * pw:(s + 1) * pw], preferred_element_type=F32)
    for s in range(conv_dim // pw):
        xpad_ref[SUBLANES:SUBLANES + rows, s * pw:(s + 1) * pw] = jnp.dot(
            hn, win_ref[:, d_inner + s * pw:d_inner + (s + 1) * pw], preferred_element_type=F32)
    dt_raw = jnp.dot(hn, win_ref[:, d_inner + conv_dim:d_inner + conv_dim + LANES], preferred_element_type=F32)

    rb = min(rows, 128)
    for s in range(conv_dim // LANES):
        sl = slice(s * LANES, (s + 1) * LANES)
        for r0 in range(0, rows, rb):
            base = xpad_ref[r0:r0 + rb + SUBLANES, sl]
            acc = cb_ref[:, sl] + cw_ref[3:4, sl] * base[SUBLANES:SUBLANES + rb, :]
            for k in range(3):
                win = pltpu.roll(base, rb + 3 - k, axis=0)
                acc = acc + cw_ref[k:k + 1, sl] * win[0:rb, :]
            xbc_ref[r0:r0 + rb, sl] = _silu(acc)
    carry = xpad_ref[rows:rows + SUBLANES, :]
    xpad_ref[0:SUBLANES, :] = carry

    dt_all = _softplus(dt_raw + dtb_ref[...])
    if n_pad:
        prow = jax.lax.broadcasted_iota(jnp.int32, dt_all.shape, 0)
        dt_all = jnp.where(prow >= n_pad, dt_all, 0.0)
    a = -jnp.exp(alog_ref[...])

    ti = jax.lax.broadcasted_iota(jnp.int32, (sub, sub), 0)
    si = jax.lax.broadcasted_iota(jnp.int32, (sub, sub), 1)
    causal = ti >= si
    lane_g = jax.lax.broadcasted_iota(jnp.int32, (sub, gw), 1)
    lane_1 = jax.lax.broadcasted_iota(jnp.int32, (1, gw), 1)

    for u in range(rows // sub):
        rs = slice(u * sub, (u + 1) * sub)
        dt = dt_all[rs, :]
        cs = _cumsum_rows(dt * a)
        cs_t = cs.T
        dt_t = dt.T
        cs_last = cs[sub - 1:sub, :]
        e_cs = jnp.exp(cs)
        w_end = dt * jnp.exp(cs_last - cs)
        e_last = jnp.exp(cs_last)
        for g in range(SSD_GROUPS):
            gsl = slice(g * gw, (g + 1) * gw)
            hsl = slice(g * HEADS_PER_GROUP, (g + 1) * HEADS_PER_GROUP)
            xg = xbc_ref[rs, gsl]
            xg_b = xg.astype(BF16)
            bg_t = xbc_ref[rs, d_inner + g * n:d_inner + (g + 1) * n].T.astype(BF16)
            c_lo = d_inner + SSD_GROUPS * n + g * n
            cg_b = xbc_ref[rs, c_lo:c_lo + n].astype(BF16)
            cb = jnp.dot(cg_b, bg_t, preferred_element_type=F32)
            st = st_ref[g]
            y = jnp.dot(cg_b, st.astype(BF16), preferred_element_type=F32) * _expand_heads4(e_cs[:, hsl], lane_g)
            for j in range(HEADS_PER_GROUP):
                h = g * HEADS_PER_GROUP + j
                seg = cs[:, h:h + 1] - cs_t[h:h + 1, :]
                m = cb * jnp.exp(jnp.where(causal, seg, -jnp.inf)) * dt_t[h:h + 1, :]
                in_head = (lane_g >= j * SSD_HEAD_DIM) & (lane_g < (j + 1) * SSD_HEAD_DIM)
                xm = jnp.where(in_head, xg_b, jnp.zeros_like(xg_b))
                y = y + jnp.dot(m.astype(BF16), xm, preferred_element_type=F32)
            y = y + dexp_ref[:, gsl] * xg
            wg = (xg * _expand_heads4(w_end[:, hsl], lane_g)).astype(BF16)
            st_ref[g] = st * _expand_heads4(e_last[:, hsl], lane_1) + jnp.dot(bg_t, wg, preferred_element_type=F32)
            y = y * _silu(z_ref[rs, gsl])
            y_ref[rs, gsl] = _rms_rows(y, nw_ref[:, gsl]).astype(BF16)

    o_ref[...] = x + jnp.dot(y_ref[...], wout_ref[...], preferred_element_type=F32)

    @pl.when(c == pl.num_programs(1) - 1)
    def _():
        for g in range(SSD_GROUPS):
            sn_ref[0, g * gw:(g + 1) * gw, :] = st_ref[g].T
        cn_ref[0] = carry


def ssd_seq(x, g_mix, w_in, conv_w, conv_b, dt_bias, a_log, d_exp, norm_w, w_out, s0, c0, nb, seq, n_pad):
    rows = SSD_CHUNK
    d = x.shape[1]
    d_inner = norm_w.shape[1]
    conv_dim = conv_w.shape[1]
    hp = s0.shape[1]
    nc = seq // rows
    shared0 = s0.shape[0] == 1
    assert w_in.shape[1] == d_inner + conv_dim + LANES

    def init_blk(shape):
        return pl.BlockSpec((1,) + shape, lambda b, c: (0 if shared0 else b, 0, 0))

    return pl.pallas_call(
        functools.partial(_ssd_seq_kernel, rows=rows, sub=SSD_SUBCHUNK, n_pad=n_pad),
        grid=(nb, nc),
        in_specs=[
            pl.BlockSpec((rows, d), lambda b, c: (b * nc + c, 0)),
            _resident((1, d)), _resident(w_in.shape),
            _resident((4, conv_dim)), _resident((1, conv_dim)), _resident((1, LANES)), _resident((1, LANES)),
            _resident((1, d_inner)), _resident((1, d_inner)), _resident(w_out.shape),
            init_blk((hp, SSD_D_STATE)), init_blk((SUBLANES, conv_dim)),
        ],
        out_specs=[
            pl.BlockSpec((rows, d), lambda b, c: (b * nc + c, 0)),
            pl.BlockSpec((1, hp, SSD_D_STATE), lambda b, c: (b, 0, 0)),
            pl.BlockSpec((1, SUBLANES, conv_dim), lambda b, c: (b, 0, 0)),
        ],
        out_shape=[
            jax.ShapeDtypeStruct((nb * seq, d), F32),
            jax.ShapeDtypeStruct((nb, hp, SSD_D_STATE), F32),
            jax.ShapeDtypeStruct((nb, SUBLANES, conv_dim), F32),
        ],
        scratch_shapes=[
            pltpu.VMEM((SSD_GROUPS, SSD_D_STATE, GROUP_WIDTH), F32),
            pltpu.VMEM((rows + SUBLANES, conv_dim), F32),
            pltpu.VMEM((rows, conv_dim), F32),
            pltpu.VMEM((rows, d_inner), F32),
            pltpu.VMEM((rows, d_inner), BF16),
        ],
        compiler_params=_cparams("parallel", "arbitrary"),
        name="ssd_seq",
    )(x, g_mix, w_in, conv_w, conv_b, dt_bias, a_log, d_exp, norm_w, w_out, s0, c0)


def _sc_seq_kernel(bg_ref, cg_ref, xv_ref, w_ref, c0_ref, h_ref, cn_ref, ppad_ref, *, rows):
    r = pl.program_id(1)

    @pl.when(r == 0)
    def _():
        ppad_ref[0:SUBLANES, :] = c0_ref[0]

    p = cg_ref[...] * xv_ref[...]
    ppad_ref[SUBLANES:SUBLANES + rows, :] = p
    v = (w_ref[0:1, :] * ppad_ref[SUBLANES - 2:SUBLANES - 2 + rows, :]
         + w_ref[1:2, :] * ppad_ref[SUBLANES - 1:SUBLANES - 1 + rows, :]
         + w_ref[2:3, :] * p)
    h_ref[...] = (bg_ref[...] * v).astype(BF16)
    carry = ppad_ref[rows:rows + SUBLANES, :]
    ppad_ref[0:SUBLANES, :] = carry

    @pl.when(r == pl.num_programs(1) - 1)
    def _():
        cn_ref[0] = carry


def sc_seq(bcx, conv_w, c0, nb, seq, tm):
    d = conv_w.shape[1]
    nr = seq // tm
    shared0 = c0.shape[0] == 1

    def row_blk(col):
        return pl.BlockSpec((tm, d), lambda b, r: (b * nr + r, col))

    return pl.pallas_call(
        functools.partial(_sc_seq_kernel, rows=tm),
        grid=(nb, nr),
        in_specs=[
            row_blk(0), row_blk(1), row_blk(2),
            pl.BlockSpec((3, d), lambda b, r: (0, 0)),
            pl.BlockSpec((1, SUBLANES, d), lambda b, r: (0 if shared0 else b, 0, 0)),
        ],
        out_specs=[
            pl.BlockSpec((tm, d), lambda b, r: (b * nr + r, 0)),
            pl.BlockSpec((1, SUBLANES, d), lambda b, r: (b, 0, 0)),
        ],
        out_shape=[
            jax.ShapeDtypeStruct((nb * seq, d), BF16),
            jax.ShapeDtypeStruct((nb, SUBLANES, d), F32),
        ],
        scratch_shapes=[pltpu.VMEM((tm + SUBLANES, d), F32)],
        compiler_params=_cparams("parallel", "arbitrary"),
        name="sc_seq",
    )(bcx, bcx, bcx, conv_w, c0)


def _cf_seq_kernel(a1_ref, a2_ref, w_ref, b_ref, lg_ref, lb_ref, c0_ref, h_ref, cn_ref, upad_ref, v_ref,
                   *, rows, taps):
    r = pl.program_id(1)
    hist = CF_CARRY_ROWS
    first = hist - (taps - 1)

    @pl.when(r == 0)
    def _():
        upad_ref[0:hist, :] = c0_ref[0]

    u = a1_ref[...] * _sigmoid(a2_ref[...])
    upad_ref[hist:hist + rows, :] = u
    d = u.shape[1]
    rb = min(rows, 128)
    for s in range(d // LANES):
        sl = slice(s * LANES, (s + 1) * LANES)
        for r0 in range(0, rows, rb):
            base = upad_ref[r0:r0 + rb + hist, sl]
            acc = jnp.broadcast_to(b_ref[:, sl], (rb, LANES))
            for phase in range(SUBLANES):
                steps = [q for q in range(hist // SUBLANES + 1) if first <= q * SUBLANES + phase <= hist]
                win = base if phase == 0 else pltpu.roll(base, rb + hist - phase, axis=0)
                for q in steps:
                    k = q * SUBLANES + phase - first
                    acc = acc + w_ref[k:k + 1, sl] * win[q * SUBLANES:q * SUBLANES + rb, :]
            v_ref[r0:r0 + rb, sl] = acc
    v = v_ref[...]
    mu = jnp.mean(v, axis=-1, keepdims=True)
    dv = v - mu
    var = jnp.mean(dv * dv, axis=-1, keepdims=True)
    y = dv * jax.lax.rsqrt(var + LN_EPS) * lg_ref[...] + lb_ref[...]
    h_ref[...] = _silu(y).astype(BF16)
    carry = upad_ref[rows:rows + hist, :]
    upad_ref[0:hist, :] = carry

    @pl.when(r == pl.num_programs(1) - 1)
    def _():
        cn_ref[0] = carry


def cf_seq(a, dw_w, dw_b, ln_g, ln_b, c0, nb, seq, tm):
    taps, d = dw_w.shape
    nr = seq // tm
    shared0 = c0.shape[0] == 1
    assert taps - 1 <= CF_CARRY_ROWS

    def row_blk(col):
        return pl.BlockSpec((tm, d), lambda b, r: (b * nr + r, col))

    def const(shape):
        return pl.BlockSpec(shape, lambda b, r: (0, 0))

    return pl.pallas_call(
        functools.partial(_cf_seq_kernel, rows=tm, taps=taps),
        grid=(nb, nr),
        in_specs=[
            row_blk(0), row_blk(1),
            const((taps, d)), const((1, d)), const((1, d)), const((1, d)),
            pl.BlockSpec((1, CF_CARRY_ROWS, d), lambda b, r: (0 if shared0 else b, 0, 0)),
        ],
        out_specs=[
            pl.BlockSpec((tm, d), lambda b, r: (b * nr + r, 0)),
            pl.BlockSpec((1, CF_CARRY_ROWS, d), lambda b, r: (b, 0, 0)),
        ],
        out_shape=[
            jax.ShapeDtypeStruct((nb * seq, d), BF16),
            jax.ShapeDtypeStruct((nb, CF_CARRY_ROWS, d), F32),
        ],
        scratch_shapes=[pltpu.VMEM((tm + CF_CARRY_ROWS, d), F32), pltpu.VMEM((tm, d), F32)],
        compiler_params=_cparams("parallel", "arbitrary"),
        name="cf_seq",
    )(a, a, dw_w, dw_b, ln_g, ln_b, c0)


def _expand_heads_mxu(v, d_inner):
    hrow = jax.lax.broadcasted_iota(jnp.int32, (LANES, d_inner), 0)
    col = jax.lax.broadcasted_iota(jnp.int32, (LANES, d_inner), 1)
    onehot = jnp.where((col >= hrow * SSD_HEAD_DIM) & (col < (hrow + 1) * SSD_HEAD_DIM), 1.0, 0.0).astype(BF16)
    hi = v.astype(BF16)
    r1 = v - hi.astype(F32)
    mid = r1.astype(BF16)
    lo = (r1 - mid.astype(F32)).astype(BF16)
    out = jnp.dot(hi, onehot, preferred_element_type=F32)
    out = out + jnp.dot(mid, onehot, preferred_element_type=F32)
    return out + jnp.dot(lo, onehot, preferred_element_type=F32)


def _ssd_dec_pre_kernel(xa_ref, xb_ref, dt_ref, cbuf_ref, cw_ref, cb_ref, dtb_ref, alog_ref,
                        xs_ref, xdt_t_ref, b_ref, c_ref, da_ref, cnew_ref):
    conv_dim = cw_ref.shape[1]
    d_inner = xs_ref.shape[1]
    gn = b_ref.shape[1]

    def conv_silu(xnew, lo, hi):
        acc = cb_ref[:, lo:hi] + cw_ref[3:4, lo:hi] * xnew
        for k in range(3):
            acc = acc + cw_ref[k:k + 1, lo:hi] * cbuf_ref[:, k * conv_dim + lo:k * conv_dim + hi]
        return _silu(acc)

    xa = xa_ref[...]
    xb = xb_ref[...]
    cnew_ref[:, 0:2 * conv_dim] = cbuf_ref[:, conv_dim:3 * conv_dim]
    cnew_ref[:, 2 * conv_dim:2 * conv_dim + d_inner] = xa
    cnew_ref[:, 2 * conv_dim + d_inner:3 * conv_dim] = xb
    dt = _softplus(dt_ref[...] + dtb_ref[...])
    da_ref[...] = jnp.exp(dt * -jnp.exp(alog_ref[...]))
    xs = conv_silu(xa, 0, d_inner)
    xs_ref[...] = xs
    xdt_t_ref[...] = (xs * _expand_heads_mxu(dt, d_inner)).T.astype(BF16)
    bc = conv_silu(xb, d_inner, conv_dim)
    b_ref[...] = bc[:, 0:gn]
    c_ref[...] = bc[:, gn:2 * gn]


def ssd_dec_pre(zx, cbuf, conv_w, conv_b, dt_bias, a_log, d_inner):
    nb = zx.shape[0]
    conv_dim = conv_w.shape[1]
    assert conv_dim == 2 * d_inner
    gn = (conv_dim - d_inner) // 2
    dt_blk = (d_inner + conv_dim) // LANES

    def full(shape):
        return pl.BlockSpec(shape, lambda i: (0, 0))

    return pl.pallas_call(
        _ssd_dec_pre_kernel,
        grid=(1,),
        in_specs=[
            pl.BlockSpec((nb, d_inner), lambda i: (0, 1)),
            pl.BlockSpec((nb, d_inner), lambda i: (0, 2)),
            pl.BlockSpec((nb, LANES), lambda i: (0, dt_blk)),
            full((nb, 3 * conv_dim)), full((4, conv_dim)), full((1, conv_dim)), full((1, LANES)), full((1, LANES)),
        ],
        out_specs=[
            full((nb, d_inner)), full((d_inner, nb)), full((nb, gn)), full((nb, gn)), full((nb, LANES)),
            full((nb, 3 * conv_dim)),
        ],
        out_shape=[
            jax.ShapeDtypeStruct((nb, d_inner), F32), jax.ShapeDtypeStruct((d_inner, nb), BF16),
            jax.ShapeDtypeStruct((nb, gn), F32), jax.ShapeDtypeStruct((nb, gn), F32),
            jax.ShapeDtypeStruct((nb, LANES), F32), jax.ShapeDtypeStruct((nb, 3 * conv_dim), F32),
        ],
        compiler_params=_cparams("arbitrary"),
        name="ssd_dec_pre",
    )(zx, zx, zx, cbuf, conv_w, conv_b, dt_bias, a_log)


def _ssd_dec_state_kernel(*refs, chained):
    if chained:
        refs = refs[1:]
    s_ref, xdt_t_ref, b_ref, c_ref, da_ref, sn_ref, y_ref = refs
    i = pl.program_id(0)
    bs = s_ref.shape[0]
    hp, nb = xdt_t_ref.shape
    n = SSD_D_STATE
    gw = GROUP_WIDTH
    groups = hp // gw
    c_blk = c_ref[...].astype(BF16)
    seq_row = jax.lax.broadcasted_iota(jnp.int32, (nb, n), 0)
    out_row = jax.lax.broadcasted_iota(jnp.int32, (bs, gw), 0)

    def one_seq(k, ys):
        da_row = da_ref[pl.ds(k, 1), :]
        new_ys = []
        for g in range(groups):
            b_sel = jnp.where(seq_row == i * bs + k, b_ref[:, g * n:(g + 1) * n], 0.0).astype(BF16)
            outer = jnp.dot(xdt_t_ref[g * gw:(g + 1) * gw, :], b_sel, preferred_element_type=F32)
            parts = []
            for j in range(HEADS_PER_GROUP):
                h = g * HEADS_PER_GROUP + j
                hsl = slice(g * gw + j * SSD_HEAD_DIM, g * gw + (j + 1) * SSD_HEAD_DIM)
                snew = s_ref[k, hsl, :] * da_row[:, h:h + 1] + outer[j * SSD_HEAD_DIM:(j + 1) * SSD_HEAD_DIM, :]
                sn_ref[k, hsl, :] = snew
                parts.append(snew.astype(BF16))
            y8 = jax.lax.dot_general(c_blk[:, g * n:(g + 1) * n], jnp.concatenate(parts, axis=0),
                                     (((1,), (1,)), ((), ())), preferred_element_type=F32)
            new_ys.append(jnp.where(out_row == k, y8, ys[g]))
        return tuple(new_ys)

    ys = jax.lax.fori_loop(0, bs, one_seq, tuple(jnp.zeros((bs, gw), F32) for _ in range(groups)))
    for g in range(groups):
        y_ref[:, g * gw:(g + 1) * gw] = ys[g]


def ssd_dec_state(states, layer, prev, xdt_t, bm, cm, da):
    n_layers, nb, hp, n = states.shape
    bs = DEC_SEQ_BLOCK
    chained = prev is not None

    def full(shape):
        return pl.BlockSpec(shape, lambda i: (0, 0))

    state_blk = pl.BlockSpec((None, bs, hp, n), lambda i: (layer, i, 0, 0))
    in_specs = [state_blk, full((hp, nb)), full(bm.shape),
                pl.BlockSpec((bs, cm.shape[1]), lambda i: (i, 0)), pl.BlockSpec((bs, LANES), lambda i: (i, 0))]
    args = [states, xdt_t, bm, cm, da]
    if chained:
        in_specs = [pl.BlockSpec(memory_space=pl.ANY)] + in_specs
        args = [prev] + args
    return pl.pallas_call(
        functools.partial(_ssd_dec_state_kernel, chained=chained),
        grid=(nb // bs,),
        in_specs=in_specs,
        out_specs=[state_blk, pl.BlockSpec((bs, hp), lambda i: (i, 0))],
        out_shape=[jax.ShapeDtypeStruct(states.shape, F32), jax.ShapeDtypeStruct((nb, hp), F32)],
        input_output_aliases={0: 0} if chained else {},
        compiler_params=_cparams("arbitrary"),
        name="ssd_dec_state",
    )(*args)


def _ssd_dec_post_kernel(y_ref, xs_ref, z_ref, dexp_ref, nw_ref, h_ref):
    gw = GROUP_WIDTH
    y = (y_ref[...] + dexp_ref[...] * xs_ref[...]) * _silu(z_ref[...])
    for g in range(y.shape[1] // gw):
        gsl = slice(g * gw, (g + 1) * gw)
        h_ref[:, gsl] = _rms_rows(y[:, gsl], nw_ref[:, gsl]).astype(BF16)


def ssd_dec_post(y, xs, zx, d_exp, norm_w):
    nb, d_inner = y.shape

    def full(shape):
        return pl.BlockSpec(shape, lambda i: (0, 0))

    return pl.pallas_call(
        _ssd_dec_post_kernel,
        grid=(1,),
        in_specs=[full((nb, d_inner)), full((nb, d_inner)), full((nb, d_inner)), full((1, d_inner)), full((1, d_inner))],
        out_specs=full((nb, d_inner)),
        out_shape=jax.ShapeDtypeStruct((nb, d_inner), BF16),
        compiler_params=_cparams("arbitrary"),
        name="ssd_dec_post",
    )(y, xs, zx, d_exp, norm_w)


def _sc_dec_kernel(bg_ref, cg_ref, xv_ref, w_ref, buf_ref, h_ref, bn_ref):
    d = w_ref.shape[1]
    p = cg_ref[...] * xv_ref[...]
    v = w_ref[0:1, :] * buf_ref[:, 0:d] + w_ref[1:2, :] * buf_ref[:, d:2 * d] + w_ref[2:3, :] * p
    h_ref[...] = (bg_ref[...] * v).astype(BF16)
    bn_ref[:, 0:d] = buf_ref[:, d:2 * d]
    bn_ref[:, d:2 * d] = p


def sc_dec(bcx, conv_w, buf):
    nb = bcx.shape[0]
    d = conv_w.shape[1]

    def full(shape, col=0):
        return pl.BlockSpec(shape, lambda i: (0, col))

    return pl.pallas_call(
        _sc_dec_kernel,
        grid=(1,),
        in_specs=[full((nb, d), 0), full((nb, d), 1), full((nb, d), 2), full((3, d)), full((nb, 2 * d))],
        out_specs=[full((nb, d)), full((nb, 2 * d))],
        out_shape=[jax.ShapeDtypeStruct((nb, d), BF16), jax.ShapeDtypeStruct((nb, 2 * d), F32)],
        compiler_params=_cparams("arbitrary"),
        name="sc_dec",
    )(bcx, bcx, bcx, conv_w, buf)


def _cf_dec_kernel(a1_ref, a2_ref, w_ref, b_ref, lg_ref, lb_ref, buf_ref, h_ref, bn_ref):
    taps, d = w_ref.shape
    u = a1_ref[...] * _sigmoid(a2_ref[...])
    v = b_ref[...] + w_ref[taps - 1:taps, :] * u
    for k in range(taps - 1):
        v = v + w_ref[k:k + 1, :] * buf_ref[:, k * d:(k + 1) * d]
    mu = jnp.mean(v, axis=-1, keepdims=True)
    dv = v - mu
    var = jnp.mean(dv * dv, axis=-1, keepdims=True)
    y = dv * jax.lax.rsqrt(var + LN_EPS) * lg_ref[...] + lb_ref[...]
    h_ref[...] = _silu(y).astype(BF16)
    bn_ref[:, 0:(taps - 2) * d] = buf_ref[:, d:(taps - 1) * d]
    bn_ref[:, (taps - 2) * d:(taps - 1) * d] = u


def cf_dec(a, dw_w, dw_b, ln_g, ln_b, buf):
    nb = a.shape[0]
    taps, d = dw_w.shape

    def full(shape, col=0):
        return pl.BlockSpec(shape, lambda i: (0, col))

    return pl.pallas_call(
        _cf_dec_kernel,
        grid=(1,),
        in_specs=[full((nb, d), 0), full((nb, d), 1), full((taps, d)), full((1, d)), full((1, d)), full((1, d)),
                  full((nb, (taps - 1) * d))],
        out_specs=[full((nb, d)), full((nb, (taps - 1) * d))],
        out_shape=[jax.ShapeDtypeStruct((nb, d), BF16), jax.ShapeDtypeStruct((nb, (taps - 1) * d), F32)],
        compiler_params=_cparams("arbitrary"),
        name="cf_dec",
    )(a, a, dw_w, dw_b, ln_g, ln_b, buf)


def kernel(x_prompt, x_sample, state_ssm, state_ssm_conv, state_sconv, state_cconv, meta_tokens,
           norm_mix, norm_ffn, norm_final, ssd_w_in, ssd_conv_w, ssd_conv_b, ssd_dt_bias, ssd_a_log,
           ssd_d, ssd_norm_w, ssd_w_out, sc_w_in, sc_conv_w, sc_w_out, cf_w_pw1, cf_b_pw1, cf_dw_w,
           cf_dw_b, cf_ln_g, cf_ln_b, cf_w_pw2, cf_b_pw2, ffn_w_gate, ffn_w_up, ffn_w_down):
    nb, seq, d = x_prompt.shape
    nd = x_sample.shape[0]
    assert x_sample.shape[1] == 1
    depth = norm_mix.shape[0]
    n_meta = meta_tokens.shape[0]
    heads = ssd_a_log.shape[1]
    d_inner = ssd_norm_w.shape[1]
    conv_dim = ssd_conv_w.shape[2]
    hp = heads * SSD_HEAD_DIM
    assert hp == d_inner and heads <= LANES and seq % SSD_CHUNK == 0 and n_meta <= SSD_CHUNK
    ssd_n = d_inner + conv_dim + LANES
    tm_main = 1024
    ns = nd + n_meta

    x_main = x_prompt.reshape(nb * seq, d)
    x_small = jnp.concatenate([x_sample.reshape(nd, d), meta_tokens.astype(F32)], axis=0)
    zeros_d = jnp.zeros((1, d), F32)
    states = state_ssm.reshape(state_ssm.shape[0], nd, hp, SSD_D_STATE)
    dec_states = None

    def row(v):
        return v.reshape(1, -1).astype(F32)

    def pad_lanes(v):
        return jnp.pad(v.reshape(1, -1).astype(F32), ((0, 0), (0, LANES - v.shape[-1])))

    p_ssm, p_ssm_conv, p_sconv, p_cconv = [], [], [], []
    s_ssm_conv, s_sconv, s_cconv = [], [], []

    for i in range(depth):
        kind, j = i % 3, i // 3
        g_mix = row(norm_mix[i])
        if kind == 0:
            w_in = jnp.pad(ssd_w_in[j], ((0, 0), (0, ssd_n - ssd_w_in.shape[2]))).astype(BF16)
            conv_w, conv_b = ssd_conv_w[j], row(ssd_conv_b[j])
            dt_bias, a_log = pad_lanes(ssd_dt_bias[j]), pad_lanes(ssd_a_log[j])
            d_exp = jnp.repeat(ssd_d[j].astype(F32), SSD_HEAD_DIM).reshape(1, d_inner)
            norm_w = row(ssd_norm_w[j])
            w_out, b_out = ssd_w_out[j].astype(BF16), zeros_d
            x_dec = x_small[:nd]
            zx_d = norm_matmul(x_dec, g_mix, w_in, jnp.zeros((1, ssd_n), F32), nd, _col_tile(ssd_n, 1024))
            cbuf = state_ssm_conv[j].reshape(nd, 3 * conv_dim)
            xs, xdt_t, bm, cm, da, cnew = ssd_dec_pre(zx_d, cbuf, conv_w, conv_b, dt_bias, a_log, d_inner)
            dec_states, y_d = ssd_dec_state(states, j, dec_states, xdt_t, bm, cm, da)
            h_d = ssd_dec_post(y_d, xs, zx_d, d_exp, norm_w)
            s_ssm_conv.append(cnew.reshape(nd, 3, conv_dim))
            x_dec = matmul_res(h_d, w_out, b_out, x_dec, nd)
            x_meta = jnp.pad(x_small[nd:], ((SSD_CHUNK - n_meta, 0), (0, 0)))
            x_meta, st_m, ct_m = ssd_seq(x_meta, g_mix, w_in, conv_w, conv_b, dt_bias, a_log, d_exp, norm_w, w_out,
                                         jnp.zeros((1, hp, SSD_D_STATE), F32),
                                         jnp.zeros((1, SUBLANES, conv_dim), F32), 1, SSD_CHUNK, SSD_CHUNK - n_meta)
            x_small = jnp.concatenate([x_dec, x_meta[SSD_CHUNK - n_meta:]], axis=0)
            x_main, st_p, ct_p = ssd_seq(x_main, g_mix, w_in, conv_w, conv_b, dt_bias, a_log, d_exp, norm_w, w_out,
                                         st_m, ct_m, nb, seq, 0)
            p_ssm.append(st_p.reshape((nb,) + state_ssm.shape[2:]))
            p_ssm_conv.append(ct_p[:, SUBLANES - 3:, :])
        else:
            if kind == 1:
                w_in, b_in = sc_w_in[j].astype(BF16), jnp.zeros((1, 3 * d), F32)
                conv_w = sc_conv_w[j]
                w_out, b_out = sc_w_out[j].astype(BF16), zeros_d
                bcx_s = norm_matmul(x_small, g_mix, w_in, b_in, ns, 1024)
                h_d, bnew = sc_dec(bcx_s[:nd], conv_w, state_sconv[j].reshape(nd, 2 * d))
                s_sconv.append(bnew.reshape(nd, 2, d))
                h_m, ct_m = sc_seq(bcx_s[nd:], conv_w, jnp.zeros((1, SUBLANES, d), F32), 1, n_meta, n_meta)
                bcx = norm_matmul(x_main, g_mix, w_in, b_in, tm_main, 1024)
                h_main, ct_p = sc_seq(bcx, conv_w, ct_m, nb, seq, 512)
                p_sconv.append(ct_p[:, SUBLANES - 2:, :])
            else:
                w_in, b_in = cf_w_pw1[j].astype(BF16), row(cf_b_pw1[j])
                dw_w, dw_b, ln_g, ln_b = cf_dw_w[j], row(cf_dw_b[j]), row(cf_ln_g[j]), row(cf_ln_b[j])
                taps = dw_w.shape[0]
                w_out, b_out = cf_w_pw2[j].astype(BF16), row(cf_b_pw2[j])
                a_s = norm_matmul(x_small, g_mix, w_in, b_in, ns, 1024)
                h_d, bnew = cf_dec(a_s[:nd], dw_w, dw_b, ln_g, ln_b, state_cconv[j].reshape(nd, (taps - 1) * d))
                s_cconv.append(bnew.reshape(nd, taps - 1, d))
                h_m, ct_m = cf_seq(a_s[nd:], dw_w, dw_b, ln_g, ln_b, jnp.zeros((1, CF_CARRY_ROWS, d), F32),
                                   1, n_meta, n_meta)
                a = norm_matmul(x_main, g_mix, w_in, b_in, tm_main, 1024)
                h_main, ct_p = cf_seq(a, dw_w, dw_b, ln_g, ln_b, ct_m, nb, seq, 256)
                p_cconv.append(ct_p[:, CF_CARRY_ROWS - (taps - 1):, :])
            x_small = matmul_res(jnp.concatenate([h_d, h_m], axis=0), w_out, b_out, x_small, ns)
            x_main = matmul_res(h_main, w_out, b_out, x_main, tm_main)

        g_ffn, g_fin = row(norm_ffn[i]), row(norm_final)
        wg, wu, wd = ffn_w_gate[i].astype(BF16), ffn_w_up[i].astype(BF16), ffn_w_down[i].astype(BF16)
        last = i == depth - 1
        x_small = ffn(x_small, g_ffn, wg, wu, wd, g_fin, ns, last)
        x_main = ffn(x_main, g_ffn, wg, wu, wd, g_fin, 512, last)

    y_prompt = x_main.reshape(nb, seq, d)
    y_sample = x_small[:nd].reshape(nd, 1, d)
    s_ssm = dec_states.reshape(state_ssm.shape)
    return (y_prompt, y_sample, jnp.stack(p_ssm), jnp.stack(p_ssm_conv), jnp.stack(p_sconv), jnp.stack(p_cconv),
            s_ssm, jnp.stack(s_ssm_conv), jnp.stack(s_sconv), jnp.stack(s_cconv))
```

```python
import functools

import jax
import jax.numpy as jnp
from jax.experimental import pallas as pl
from jax.experimental.pallas import tpu as pltpu

F32 = jnp.float32
BF16 = jnp.bfloat16

EPS = 1e-6
LN_EPS = 1e-5
SSD_HEAD_DIM = 64
SSD_D_STATE = 128
SSD_GROUPS = 8
SSD_CHUNK = 256
SSD_SUBCHUNK = 128
HEADS_PER_GROUP = 4
GROUP_WIDTH = HEADS_PER_GROUP * SSD_HEAD_DIM
LANES = 128
SUBLANES = 8
MXU_WIDTH = 256
CF_CARRY_ROWS = 32
DEC_SEQ_BLOCK = 8
LAYER_ROWS = 512
VMEM_LIMIT_BYTES = 56 * 1024 * 1024
SSD_LAYER_VMEM_BYTES = 60 * 1024 * 1024


def _cparams(*sem, vmem=VMEM_LIMIT_BYTES):
    return pltpu.CompilerParams(dimension_semantics=sem, vmem_limit_bytes=vmem)


def _sigmoid(x):
    return 1.0 / (1.0 + jnp.exp(-x))


def _silu(x):
    return x * _sigmoid(x)


def _softplus(x):
    return jnp.maximum(x, 0.0) + jnp.log1p(jnp.exp(-jnp.abs(x)))


def _rms_rows(x, g):
    return x * jax.lax.rsqrt(jnp.mean(x * x, axis=-1, keepdims=True) + EPS) * g


def _col_tile(n, limit):
    return max(t for t in range(LANES, limit + 1, LANES) if n % t == 0)


def _ff_chunks(dff):
    step = 2 * MXU_WIDTH
    return tuple((lo, min(lo + step, dff)) for lo in range(0, dff, step))


class _Stacked:
    def __init__(self, array, layer):
        self.array, self.layer, self.shape = array, layer, array.shape[1:]


def _operand(w):
    return w.array if isinstance(w, _Stacked) else w


def _resident(w):
    if isinstance(w, _Stacked):
        return pl.BlockSpec((None,) + w.shape, lambda *_: (w.layer,) + (0,) * len(w.shape),
                            pipeline_mode=pl.Buffered(1))
    return pl.BlockSpec(w.shape, lambda *_: (0,) * len(w.shape), pipeline_mode=pl.Buffered(1))


def _norm_matmul_kernel(x_ref, g_ref, w_ref, b_ref, o_ref, hn_ref):
    @pl.when(pl.program_id(1) == 0)
    def _():
        hn_ref[...] = _rms_rows(x_ref[...], g_ref[...]).astype(BF16)

    o_ref[...] = jnp.dot(hn_ref[...], w_ref[...], preferred_element_type=F32) + b_ref[...]


def norm_matmul(x, g, w, b, tm, tn):
    m, d = x.shape
    n = w.shape[1]
    if isinstance(w, _Stacked):
        w_spec = pl.BlockSpec((None, d, tn), lambda i, j: (w.layer, 0, j))
    else:
        w_spec = pl.BlockSpec((d, tn), lambda i, j: (0, j))
    return pl.pallas_call(
        _norm_matmul_kernel,
        grid=(m // tm, n // tn),
        in_specs=[
            pl.BlockSpec((tm, d), lambda i, j: (i, 0)),
            pl.BlockSpec((1, d), lambda i, j: (0, 0)),
            w_spec,
            pl.BlockSpec((1, tn), lambda i, j: (0, j)),
        ],
        out_specs=pl.BlockSpec((tm, tn), lambda i, j: (i, j)),
        out_shape=jax.ShapeDtypeStruct((m, n), F32),
        scratch_shapes=[pltpu.VMEM((tm, d), BF16)],
        compiler_params=_cparams("parallel", "arbitrary"),
        name="norm_matmul",
    )(x, g, _operand(w), b)


def _matmul_res_kernel(h_ref, w_ref, b_ref, x_ref, o_ref):
    o_ref[...] = x_ref[...] + b_ref[...] + jnp.dot(h_ref[...], w_ref[...], preferred_element_type=F32)


def matmul_res(h, w, b, x, tm):
    m, k = h.shape
    d = w.shape[1]
    return pl.pallas_call(
        _matmul_res_kernel,
        grid=(m // tm,),
        in_specs=[
            pl.BlockSpec((tm, k), lambda i: (i, 0)),
            _resident(w),
            _resident(b),
            pl.BlockSpec((tm, d), lambda i: (i, 0)),
        ],
        out_specs=pl.BlockSpec((tm, d), lambda i: (i, 0)),
        out_shape=jax.ShapeDtypeStruct((m, d), F32),
        compiler_params=_cparams("parallel"),
        name="matmul_res",
    )(h, _operand(w), b, x)


def _ffn_steps(x_ref, g_ref, wg_ref, wu_ref, wd_ref, gf_ref, o_ref, acc_ref, final_norm):
    held = {}

    def start():
        x = x_ref[...]
        held["hn"] = _rms_rows(x, g_ref[...]).astype(BF16)
        acc_ref[...] = x

    def chunk(lo, hi):
        def run():
            gate = jnp.dot(held["hn"], wg_ref[:, lo:hi], preferred_element_type=F32)
            up = jnp.dot(held["hn"], wu_ref[:, lo:hi], preferred_element_type=F32)
            act = (_silu(gate) * up).astype(BF16)
            acc_ref[...] += jnp.dot(act, wd_ref[lo:hi, :], preferred_element_type=F32)
        return run

    def finish():
        y = acc_ref[...]
        if final_norm:
            y = _rms_rows(y, gf_ref[...])
        o_ref[...] = y

    return [start] + [chunk(lo, hi) for lo, hi in _ff_chunks(wg_ref.shape[1])] + [finish]


class _Drip:
    def __init__(self, steps, slots):
        self.steps, self.slots, self.ticks, self.done = steps, slots, 0, 0

    def tick(self):
        self.ticks += 1
        self._run_to(self.ticks * len(self.steps) // self.slots)

    def flush(self):
        self._run_to(len(self.steps))

    def _run_to(self, target):
        while self.done < min(target, len(self.steps)):
            self.steps[self.done]()
            self.done += 1


def _ffn_kernel(x_ref, g_ref, wg_ref, wu_ref, wd_ref, gf_ref, o_ref, acc_ref, *, final_norm):
    for step in _ffn_steps(x_ref, g_ref, wg_ref, wu_ref, wd_ref, gf_ref, o_ref, acc_ref, final_norm):
        step()


def _resident_all(ws):
    return [_resident(w) for w in ws]


def _operands(ws):
    return [_operand(w) for w in ws]


def ffn(x, ffn_w, tm, final_norm):
    m, d = x.shape
    dff = ffn_w[1].shape[1]
    return pl.pallas_call(
        functools.partial(_ffn_kernel, final_norm=final_norm),
        grid=(m // tm,),
        in_specs=[pl.BlockSpec((tm, d), lambda i: (i, 0))] + _resident_all(ffn_w),
        out_specs=pl.BlockSpec((tm, d), lambda i: (i, 0)),
        out_shape=jax.ShapeDtypeStruct((m, d), F32),
        scratch_shapes=[pltpu.VMEM((tm, d), F32)],
        compiler_params=_cparams("parallel"),
        name="ffn",
    )(x, *_operands(ffn_w))


def _lag_specs(tiles, rows, d):
    x_spec = pl.BlockSpec((rows, d), lambda t: (jnp.minimum(t, tiles - 1), 0))
    o_spec = pl.BlockSpec((rows, d), lambda t: (jnp.maximum(t - 1, 0), 0))
    return x_spec, o_spec


def _seq_out_spec(shape, tiles, per_seq):
    return pl.BlockSpec((1,) + shape, lambda t: (jnp.minimum(t, tiles - 1) // per_seq, 0, 0))


def _seq_init_spec(shape, shared, tiles, per_seq):
    if shared:
        return pl.BlockSpec((1,) + shape, lambda t: (0, 0, 0), pipeline_mode=pl.Buffered(1))
    return _seq_out_spec(shape, tiles, per_seq)


def _cumsum_rows(x):
    rows = x.shape[0]
    row = jax.lax.broadcasted_iota(jnp.int32, x.shape, 0)
    shift = 1
    while shift < rows:
        x = x + jnp.where(row >= shift, pltpu.roll(x, shift, axis=0), 0.0)
        shift *= 2
    return x


def _expand_heads4(cols, lane):
    out = cols[:, 3:4]
    for j in (2, 1, 0):
        out = jnp.where(lane < (j + 1) * SSD_HEAD_DIM, cols[:, j:j + 1], out)
    return out


def _ssd_layer_kernel(x_ref, gm_ref, win_ref, cw_ref, cb_ref, dtb_ref, alog_ref, dexp_ref, nw_ref, wout_ref,
                      s0_ref, c0_ref, gf_ref, wg_ref, wu_ref, wd_ref, gl_ref,
                      o_ref, sn_ref, cn_ref,
                      xmid_ref, acc_ref, st_ref, xpad_ref, z_ref, y_ref,
                      *, rows, sub, n_pad, tiles, per_seq, final_norm):
    t = pl.program_id(0)
    c = jax.lax.rem(t, per_seq)
    d_inner = z_ref.shape[1]
    conv_dim = xpad_ref.shape[1]
    n = SSD_D_STATE
    gw = GROUP_WIDTH
    pw = 2 * MXU_WIDTH

    @pl.when(t == 0)
    def _():
        xmid_ref[...] = jnp.zeros_like(xmid_ref)

    @pl.when(c == 0)
    def _():
        for g in range(SSD_GROUPS):
            st_ref[g] = s0_ref[0, g * gw:(g + 1) * gw, :].T
        xpad_ref[0:SUBLANES, :] = c0_ref[0]

    conv_tick = 4
    drip = _Drip(_ffn_steps(xmid_ref, gf_ref, wg_ref, wu_ref, wd_ref, gl_ref, o_ref, acc_ref, final_norm),
                 conv_dim // LANES // conv_tick + (rows // sub) * SSD_GROUPS)

    x = x_ref[...]
    hn = _rms_rows(x, gm_ref[...]).astype(BF16)
    for s in range(d_inner // pw):
        z_ref[:, s * pw:(s + 1) * pw] = jnp.dot(hn, win_ref[:, s * pw:(s + 1) * pw], preferred_element_type=F32)
    for s in range(conv_dim // pw):
        xpad_ref[SUBLANES:SUBLANES + rows, s * pw:(s + 1) * pw] = jnp.dot(
            hn, win_ref[:, d_inner + s * pw:d_inner + (s + 1) * pw], preferred_element_type=F32)
    dt_raw = jnp.dot(hn, win_ref[:, d_inner + conv_dim:d_inner + conv_dim + LANES], preferred_element_type=F32)

    carry = xpad_ref[rows:rows + SUBLANES, :]
    rb = min(rows, 128)
    for s in range(conv_dim // LANES):
        sl = slice(s * LANES, (s + 1) * LANES)
        for r0 in reversed(range(0, rows, rb)):
            base = xpad_ref[r0:r0 + rb + SUBLANES, sl]
            acc = cb_ref[:, sl] + cw_ref[3:4, sl] * base[SUBLANES:SUBLANES + rb, :]
            for k in range(3):
                win = pltpu.roll(base, rb + 3 - k, axis=0)
                acc = acc + cw_ref[k:k + 1, sl] * win[0:rb, :]
            xpad_ref[SUBLANES + r0:SUBLANES + r0 + rb, sl] = _silu(acc)
        if s % conv_tick == conv_tick - 1:
            drip.tick()
    xpad_ref[0:SUBLANES, :] = carry

    dt_all = _softplus(dt_raw + dtb_ref[...])
    if n_pad:
        prow = jax.lax.broadcasted_iota(jnp.int32, dt_all.shape, 0)
        dt_all = jnp.where(prow >= n_pad, dt_all, 0.0)
    a = -jnp.exp(alog_ref[...])

    ti = jax.lax.broadcasted_iota(jnp.int32, (sub, sub), 0)
    si = jax.lax.broadcasted_iota(jnp.int32, (sub, sub), 1)
    causal = ti >= si
    lane_g = jax.lax.broadcasted_iota(jnp.int32, (sub, gw), 1)
    lane_1 = jax.lax.broadcasted_iota(jnp.int32, (1, gw), 1)

    for u in range(rows // sub):
        rs = slice(SUBLANES + u * sub, SUBLANES + (u + 1) * sub)
        us = slice(u * sub, (u + 1) * sub)
        dt = dt_all[us, :]
        cs = _cumsum_rows(dt * a)
        cs_t = cs.T
        dt_t = dt.T
        cs_last = cs[sub - 1:sub, :]
        e_cs = jnp.exp(cs)
        w_end = dt * jnp.exp(cs_last - cs)
        e_last = jnp.exp(cs_last)
        for g in range(SSD_GROUPS):
            gsl = slice(g * gw, (g + 1) * gw)
            hsl = slice(g * HEADS_PER_GROUP, (g + 1) * HEADS_PER_GROUP)
            xg = xpad_ref[rs, gsl]
            xg_b = xg.astype(BF16)
            bg_t = xpad_ref[rs, d_inner + g * n:d_inner + (g + 1) * n].T.astype(BF16)
            c_lo = d_inner + SSD_GROUPS * n + g * n
            cg_b = xpad_ref[rs, c_lo:c_lo + n].astype(BF16)
            cb = jnp.dot(cg_b, bg_t, preferred_element_type=F32)
            st = st_ref[g]
            y = jnp.dot(cg_b, st.astype(BF16), preferred_element_type=F32) * _expand_heads4(e_cs[:, hsl], lane_g)
            for j in range(HEADS_PER_GROUP):
                h = g * HEADS_PER_GROUP + j
                seg = cs[:, h:h + 1] - cs_t[h:h + 1, :]
                m = cb * jnp.exp(jnp.where(causal, seg, -jnp.inf)) * dt_t[h:h + 1, :]
                in_head = (lane_g >= j * SSD_HEAD_DIM) & (lane_g < (j + 1) * SSD_HEAD_DIM)
                xm = jnp.where(in_head, xg_b, jnp.zeros_like(xg_b))
                y = y + jnp.dot(m.astype(BF16), xm, preferred_element_type=F32)
            y = y + dexp_ref[:, gsl] * xg
            wg = (xg * _expand_heads4(w_end[:, hsl], lane_g)).astype(BF16)
            st_ref[g] = st * _expand_heads4(e_last[:, hsl], lane_1) + jnp.dot(bg_t, wg, preferred_element_type=F32)
            y = y * _silu(z_ref[us, gsl])
            y_ref[us, gsl] = _rms_rows(y, nw_ref[:, gsl]).astype(BF16)
            drip.tick()

    drip.flush()
    xmid_ref[...] = x + jnp.dot(y_ref[...], wout_ref[...], preferred_element_type=F32)

    @pl.when((c == per_seq - 1) & (t < tiles))
    def _():
        for g in range(SSD_GROUPS):
            sn_ref[0, g * gw:(g + 1) * gw, :] = st_ref[g].T
        cn_ref[0] = carry


def ssd_layer(x, mix_w, ffn_w, s0, c0, nb, seq, n_pad, final_norm):
    rows = SSD_CHUNK
    d = x.shape[1]
    w_in, conv_w, norm_w, w_out = mix_w[1], mix_w[2], mix_w[7], mix_w[8]
    d_inner = norm_w.shape[1]
    conv_dim = conv_w.shape[1]
    dff = ffn_w[1].shape[1]
    hp = s0.shape[1]
    per_seq = seq // rows
    tiles = nb * per_seq
    shared0 = s0.shape[0] == 1
    assert w_in.shape[1] == d_inner + conv_dim + LANES
    x_spec, o_spec = _lag_specs(tiles, rows, d)
    return pl.pallas_call(
        functools.partial(_ssd_layer_kernel, rows=rows, sub=SSD_SUBCHUNK, n_pad=n_pad, tiles=tiles, per_seq=per_seq,
                          final_norm=final_norm),
        grid=(tiles + 1,),
        in_specs=[x_spec] + _resident_all(mix_w) + [
            _seq_init_spec((hp, SSD_D_STATE), shared0, tiles, per_seq),
            _seq_init_spec((SUBLANES, conv_dim), shared0, tiles, per_seq),
        ] + _resident_all(ffn_w),
        out_specs=[
            o_spec,
            _seq_out_spec((hp, SSD_D_STATE), tiles, per_seq),
            _seq_out_spec((SUBLANES, conv_dim), tiles, per_seq),
        ],
        out_shape=[
            jax.ShapeDtypeStruct((nb * seq, d), F32),
            jax.ShapeDtypeStruct((nb, hp, SSD_D_STATE), F32),
            jax.ShapeDtypeStruct((nb, SUBLANES, conv_dim), F32),
        ],
        scratch_shapes=[
            pltpu.VMEM((rows, d), F32),
            pltpu.VMEM((rows, d), F32),
            pltpu.VMEM((SSD_GROUPS, SSD_D_STATE, GROUP_WIDTH), F32),
            pltpu.VMEM((rows + SUBLANES, conv_dim), F32),
            pltpu.VMEM((rows, d_inner), F32),
            pltpu.VMEM((rows, d_inner), BF16),
        ],
        compiler_params=_cparams("arbitrary", vmem=SSD_LAYER_VMEM_BYTES),
        name="ssd_layer",
    )(x, *_operands(mix_w), s0, c0, *_operands(ffn_w))


def _sc_layer_kernel(x_ref, gm_ref, win_ref, cw_ref, wout_ref, c0_ref, gf_ref, wg_ref, wu_ref, wd_ref, gl_ref,
                     o_ref, cn_ref, xmid_ref, acc_ref, ppad_ref, *, rows, tiles, per_seq, final_norm):
    t = pl.program_id(0)
    c = jax.lax.rem(t, per_seq)

    @pl.when(t == 0)
    def _():
        xmid_ref[...] = jnp.zeros_like(xmid_ref)

    @pl.when(c == 0)
    def _():
        ppad_ref[0:SUBLANES, :] = c0_ref[0]

    for step in _ffn_steps(xmid_ref, gf_ref, wg_ref, wu_ref, wd_ref, gl_ref, o_ref, acc_ref, final_norm):
        step()

    x = x_ref[...]
    d = x.shape[1]
    hn = _rms_rows(x, gm_ref[...]).astype(BF16)
    p = (jnp.dot(hn, win_ref[:, d:2 * d], preferred_element_type=F32)
         * jnp.dot(hn, win_ref[:, 2 * d:3 * d], preferred_element_type=F32))
    ppad_ref[SUBLANES:SUBLANES + rows, :] = p
    v = (cw_ref[0:1, :] * ppad_ref[SUBLANES - 2:SUBLANES - 2 + rows, :]
         + cw_ref[1:2, :] * ppad_ref[SUBLANES - 1:SUBLANES - 1 + rows, :]
         + cw_ref[2:3, :] * p)
    h = (jnp.dot(hn, win_ref[:, 0:d], preferred_element_type=F32) * v).astype(BF16)
    xmid_ref[...] = x + jnp.dot(h, wout_ref[...], preferred_element_type=F32)
    carry = ppad_ref[rows:rows + SUBLANES, :]
    ppad_ref[0:SUBLANES, :] = carry

    @pl.when((c == per_seq - 1) & (t < tiles))
    def _():
        cn_ref[0] = carry


def sc_layer(x, mix_w, ffn_w, c0, nb, seq, rows, final_norm):
    d = x.shape[1]
    dff = ffn_w[1].shape[1]
    per_seq = seq // rows
    tiles = nb * per_seq
    x_spec, o_spec = _lag_specs(tiles, rows, d)
    return pl.pallas_call(
        functools.partial(_sc_layer_kernel, rows=rows, tiles=tiles, per_seq=per_seq, final_norm=final_norm),
        grid=(tiles + 1,),
        in_specs=[x_spec] + _resident_all(mix_w) + [
            _seq_init_spec((SUBLANES, d), c0.shape[0] == 1, tiles, per_seq),
        ] + _resident_all(ffn_w),
        out_specs=[o_spec, _seq_out_spec((SUBLANES, d), tiles, per_seq)],
        out_shape=[jax.ShapeDtypeStruct((nb * seq, d), F32), jax.ShapeDtypeStruct((nb, SUBLANES, d), F32)],
        scratch_shapes=[pltpu.VMEM((rows, d), F32), pltpu.VMEM((rows, d), F32), pltpu.VMEM((rows + SUBLANES, d), F32)],
        compiler_params=_cparams("arbitrary"),
        name="sc_layer",
    )(x, *_operands(mix_w), c0, *_operands(ffn_w))


def _cf_layer_kernel(x_ref, gm_ref, w1_ref, b1_ref, dw_ref, db_ref, lg_ref, lb_ref, w2_ref, b2_ref, c0_ref,
                     gf_ref, wg_ref, wu_ref, wd_ref, gl_ref,
                     o_ref, cn_ref, xmid_ref, acc_ref, upad_ref, v_ref, *, rows, tiles, per_seq, final_norm):
    t = pl.program_id(0)
    c = jax.lax.rem(t, per_seq)
    taps = dw_ref.shape[0]
    hist = CF_CARRY_ROWS
    first = hist - (taps - 1)

    @pl.when(t == 0)
    def _():
        xmid_ref[...] = jnp.zeros_like(xmid_ref)

    @pl.when(c == 0)
    def _():
        upad_ref[0:hist, :] = c0_ref[0]

    x = x_ref[...]
    d = x.shape[1]
    rb = min(rows, 128)
    drip = _Drip(_ffn_steps(xmid_ref, gf_ref, wg_ref, wu_ref, wd_ref, gl_ref, o_ref, acc_ref, final_norm),
                 (d // LANES) * (rows // rb))
    hn = _rms_rows(x, gm_ref[...]).astype(BF16)
    u = ((jnp.dot(hn, w1_ref[:, 0:d], preferred_element_type=F32) + b1_ref[:, 0:d])
         * _sigmoid(jnp.dot(hn, w1_ref[:, d:2 * d], preferred_element_type=F32) + b1_ref[:, d:2 * d]))
    upad_ref[hist:hist + rows, :] = u
    for s in range(d // LANES):
        sl = slice(s * LANES, (s + 1) * LANES)
        for r0 in range(0, rows, rb):
            base = upad_ref[r0:r0 + rb + hist, sl]
            acc = jnp.broadcast_to(db_ref[:, sl], (rb, LANES))
            for phase in range(SUBLANES):
                steps = [q for q in range(hist // SUBLANES + 1) if first <= q * SUBLANES + phase <= hist]
                win = base if phase == 0 else pltpu.roll(base, rb + hist - phase, axis=0)
                for q in steps:
                    k = q * SUBLANES + phase - first
                    acc = acc + dw_ref[k:k + 1, sl] * win[q * SUBLANES:q * SUBLANES + rb, :]
            v_ref[r0:r0 + rb, sl] = acc
            drip.tick()
    drip.flush()
    v = v_ref[...]
    mu = jnp.mean(v, axis=-1, keepdims=True)
    dv = v - mu
    var = jnp.mean(dv * dv, axis=-1, keepdims=True)
    y = dv * jax.lax.rsqrt(var + LN_EPS) * lg_ref[...] + lb_ref[...]
    h = _silu(y).astype(BF16)
    xmid_ref[...] = x + b2_ref[...] + jnp.dot(h, w2_ref[...], preferred_element_type=F32)
    carry = upad_ref[rows:rows + hist, :]
    upad_ref[0:hist, :] = carry

    @pl.when((c == per_seq - 1) & (t < tiles))
    def _():
        cn_ref[0] = carry


def cf_layer(x, mix_w, ffn_w, c0, nb, seq, rows, final_norm):
    d = x.shape[1]
    dff = ffn_w[1].shape[1]
    per_seq = seq // rows
    tiles = nb * per_seq
    assert mix_w[3].shape[0] - 1 <= CF_CARRY_ROWS
    x_spec, o_spec = _lag_specs(tiles, rows, d)
    return pl.pallas_call(
        functools.partial(_cf_layer_kernel, rows=rows, tiles=tiles, per_seq=per_seq, final_norm=final_norm),
        grid=(tiles + 1,),
        in_specs=[x_spec] + _resident_all(mix_w) + [
            _seq_init_spec((CF_CARRY_ROWS, d), c0.shape[0] == 1, tiles, per_seq),
        ] + _resident_all(ffn_w),
        out_specs=[o_spec, _seq_out_spec((CF_CARRY_ROWS, d), tiles, per_seq)],
        out_shape=[jax.ShapeDtypeStruct((nb * seq, d), F32), jax.ShapeDtypeStruct((nb, CF_CARRY_ROWS, d), F32)],
        scratch_shapes=[pltpu.VMEM((rows, d), F32), pltpu.VMEM((rows, d), F32),
                        pltpu.VMEM((rows + CF_CARRY_ROWS, d), F32), pltpu.VMEM((rows, d), F32)],
        compiler_params=_cparams("arbitrary"),
        name="cf_layer",
    )(x, *_operands(mix_w), c0, *_operands(ffn_w))


def _expand_heads_mxu(v, d_inner):
    hrow = jax.lax.broadcasted_iota(jnp.int32, (LANES, d_inner), 0)
    col = jax.lax.broadcasted_iota(jnp.int32, (LANES, d_inner), 1)
    onehot = jnp.where((col >= hrow * SSD_HEAD_DIM) & (col < (hrow + 1) * SSD_HEAD_DIM), 1.0, 0.0).astype(BF16)
    hi = v.astype(BF16)
    r1 = v - hi.astype(F32)
    mid = r1.astype(BF16)
    lo = (r1 - mid.astype(F32)).astype(BF16)
    out = jnp.dot(hi, onehot, preferred_element_type=F32)
    out = out + jnp.dot(mid, onehot, preferred_element_type=F32)
    return out + jnp.dot(lo, onehot, preferred_element_type=F32)


def _ssd_dec_pre_kernel(xa_ref, xb_ref, dt_ref, cbuf_ref, cw_ref, cb_ref, dtb_ref, alog_ref,
                        xs_ref, xdt_t_ref, b_ref, c_ref, da_ref, cnew_ref):
    conv_dim = cw_ref.shape[1]
    d_inner = xs_ref.shape[1]
    gn = b_ref.shape[1]

    def conv_silu(xnew, lo, hi):
        acc = cb_ref[:, lo:hi] + cw_ref[3:4, lo:hi] * xnew
        for k in range(3):
            acc = acc + cw_ref[k:k + 1, lo:hi] * cbuf_ref[:, k * conv_dim + lo:k * conv_dim + hi]
        return _silu(acc)

    xa = xa_ref[...]
    xb = xb_ref[...]
    cnew_ref[:, 0:2 * conv_dim] = cbuf_ref[:, conv_dim:3 * conv_dim]
    cnew_ref[:, 2 * conv_dim:2 * conv_dim + d_inner] = xa
    cnew_ref[:, 2 * conv_dim + d_inner:3 * conv_dim] = xb
    dt = _softplus(dt_ref[...] + dtb_ref[...])
    da_ref[...] = jnp.exp(dt * -jnp.exp(alog_ref[...]))
    xs = conv_silu(xa, 0, d_inner)
    xs_ref[...] = xs
    xdt_t_ref[...] = (xs * _expand_heads_mxu(dt, d_inner)).T.astype(BF16)
    bc = conv_silu(xb, d_inner, conv_dim)
    b_ref[...] = bc[:, 0:gn]
    c_ref[...] = bc[:, gn:2 * gn]


def ssd_dec_pre(zx, cbuf, conv_w, conv_b, dt_bias, a_log, d_inner):
    nb = zx.shape[0]
    conv_dim = conv_w.shape[1]
    assert conv_dim == 2 * d_inner
    gn = (conv_dim - d_inner) // 2
    dt_blk = (d_inner + conv_dim) // LANES

    def full(shape):
        return pl.BlockSpec(shape, lambda i: (0, 0))

    return pl.pallas_call(
        _ssd_dec_pre_kernel,
        grid=(1,),
        in_specs=[
            pl.BlockSpec((nb, d_inner), lambda i: (0, 1)),
            pl.BlockSpec((nb, d_inner), lambda i: (0, 2)),
            pl.BlockSpec((nb, LANES), lambda i: (0, dt_blk)),
            full((nb, 3 * conv_dim)), full((4, conv_dim)), full((1, conv_dim)), full((1, LANES)), full((1, LANES)),
        ],
        out_specs=[
            full((nb, d_inner)), full((d_inner, nb)), full((nb, gn)), full((nb, gn)), full((nb, LANES)),
            full((nb, 3 * conv_dim)),
        ],
        out_shape=[
            jax.ShapeDtypeStruct((nb, d_inner), F32), jax.ShapeDtypeStruct((d_inner, nb), BF16),
            jax.ShapeDtypeStruct((nb, gn), F32), jax.ShapeDtypeStruct((nb, gn), F32),
            jax.ShapeDtypeStruct((nb, LANES), F32), jax.ShapeDtypeStruct((nb, 3 * conv_dim), F32),
        ],
        compiler_params=_cparams("arbitrary"),
        name="ssd_dec_pre",
    )(zx, zx, zx, cbuf, conv_w, conv_b, dt_bias, a_log)


def _ssd_dec_state_kernel(*refs, chained):
    if chained:
        refs = refs[1:]
    s_ref, xdt_t_ref, b_ref, c_ref, da_ref, sn_ref, y_ref = refs
    i = pl.program_id(0)
    bs = s_ref.shape[0]
    hp, nb = xdt_t_ref.shape
    n = SSD_D_STATE
    gw = GROUP_WIDTH
    groups = hp // gw
    c_blk = c_ref[...].astype(BF16)
    seq_row = jax.lax.broadcasted_iota(jnp.int32, (nb, n), 0)
    out_row = jax.lax.broadcasted_iota(jnp.int32, (bs, gw), 0)

    def one_seq(k, ys):
        da_row = da_ref[pl.ds(k, 1), :]
        new_ys = []
        for g in range(groups):
            b_sel = jnp.where(seq_row == i * bs + k, b_ref[:, g * n:(g + 1) * n], 0.0).astype(BF16)
            outer = jnp.dot(xdt_t_ref[g * gw:(g + 1) * gw, :], b_sel, preferred_element_type=F32)
            parts = []
            for j in range(HEADS_PER_GROUP):
                h = g * HEADS_PER_GROUP + j
                hsl = slice(g * gw + j * SSD_HEAD_DIM, g * gw + (j + 1) * SSD_HEAD_DIM)
                snew = s_ref[k, hsl, :] * da_row[:, h:h + 1] + outer[j * SSD_HEAD_DIM:(j + 1) * SSD_HEAD_DIM, :]
                sn_ref[k, hsl, :] = snew
                parts.append(snew.astype(BF16))
            y8 = jax.lax.dot_general(c_blk[:, g * n:(g + 1) * n], jnp.concatenate(parts, axis=0),
                                     (((1,), (1,)), ((), ())), preferred_element_type=F32)
            new_ys.append(jnp.where(out_row == k, y8, ys[g]))
        return tuple(new_ys)

    ys = jax.lax.fori_loop(0, bs, one_seq, tuple(jnp.zeros((bs, gw), F32) for _ in range(groups)))
    for g in range(groups):
        y_ref[:, g * gw:(g + 1) * gw] = ys[g]


def ssd_dec_state(states, layer, prev, xdt_t, bm, cm, da):
    n_layers, nb, hp, n = states.shape
    bs = DEC_SEQ_BLOCK
    chained = prev is not None

    def full(shape):
        return pl.BlockSpec(shape, lambda i: (0, 0))

    state_blk = pl.BlockSpec((None, bs, hp, n), lambda i: (layer, i, 0, 0))
    in_specs = [state_blk, full((hp, nb)), full(bm.shape),
                pl.BlockSpec((bs, cm.shape[1]), lambda i: (i, 0)), pl.BlockSpec((bs, LANES), lambda i: (i, 0))]
    args = [states, xdt_t, bm, cm, da]
    if chained:
        in_specs = [pl.BlockSpec(memory_space=pl.ANY)] + in_specs
        args = [prev] + args
    return pl.pallas_call(
        functools.partial(_ssd_dec_state_kernel, chained=chained),
        grid=(nb // bs,),
        in_specs=in_specs,
        out_specs=[state_blk, pl.BlockSpec((bs, hp), lambda i: (i, 0))],
        out_shape=[jax.ShapeDtypeStruct(states.shape, F32), jax.ShapeDtypeStruct((nb, hp), F32)],
        input_output_aliases={0: 0} if chained else {},
        compiler_params=_cparams("arbitrary"),
        name="ssd_dec_state",
    )(*args)


def _ssd_dec_post_kernel(y_ref, xs_ref, z_ref, dexp_ref, nw_ref, h_ref):
    gw = GROUP_WIDTH
    y = (y_ref[...] + dexp_ref[...] * xs_ref[...]) * _silu(z_ref[...])
    for g in range(y.shape[1] // gw):
        gsl = slice(g * gw, (g + 1) * gw)
        h_ref[:, gsl] = _rms_rows(y[:, gsl], nw_ref[:, gsl]).astype(BF16)


def ssd_dec_post(y, xs, zx, d_exp, norm_w):
    nb, d_inner = y.shape

    def full(shape):
        return pl.BlockSpec(shape, lambda i: (0, 0))

    return pl.pallas_call(
        _ssd_dec_post_kernel,
        grid=(1,),
        in_specs=[full((nb, d_inner)), full((nb, d_inner)), full((nb, d_inner)), full((1, d_inner)), full((1, d_inner))],
        out_specs=full((nb, d_inner)),
        out_shape=jax.ShapeDtypeStruct((nb, d_inner), BF16),
        compiler_params=_cparams("arbitrary"),
        name="ssd_dec_post",
    )(y, xs, zx, d_exp, norm_w)


def _sc_dec_kernel(bg_ref, cg_ref, xv_ref, w_ref, buf_ref, h_ref, bn_ref):
    d = w_ref.shape[1]
    p = cg_ref[...] * xv_ref[...]
    v = w_ref[0:1, :] * buf_ref[:, 0:d] + w_ref[1:2, :] * buf_ref[:, d:2 * d] + w_ref[2:3, :] * p
    h_ref[...] = (bg_ref[...] * v).astype(BF16)
    bn_ref[:, 0:d] = buf_ref[:, d:2 * d]
    bn_ref[:, d:2 * d] = p


def sc_dec(bcx, conv_w, buf):
    nb = bcx.shape[0]
    d = conv_w.shape[1]

    def full(shape, col=0):
        return pl.BlockSpec(shape, lambda i: (0, col))

    return pl.pallas_call(
        _sc_dec_kernel,
        grid=(1,),
        in_specs=[full((nb, d), 0), full((nb, d), 1), full((nb, d), 2), full((3, d)), full((nb, 2 * d))],
        out_specs=[full((nb, d)), full((nb, 2 * d))],
        out_shape=[jax.ShapeDtypeStruct((nb, d), BF16), jax.ShapeDtypeStruct((nb, 2 * d), F32)],
        compiler_params=_cparams("arbitrary"),
        name="sc_dec",
    )(bcx, bcx, bcx, conv_w, buf)


def _cf_dec_kernel(a1_ref, a2_ref, w_ref, b_ref, lg_ref, lb_ref, buf_ref, h_ref, bn_ref):
    taps, d = w_ref.shape
    u = a1_ref[...] * _sigmoid(a2_ref[...])
    v = b_ref[...] + w_ref[taps - 1:taps, :] * u
    for k in range(taps - 1):
        v = v + w_ref[k:k + 1, :] * buf_ref[:, k * d:(k + 1) * d]
    mu = jnp.mean(v, axis=-1, keepdims=True)
    dv = v - mu
    var = jnp.mean(dv * dv, axis=-1, keepdims=True)
    y = dv * jax.lax.rsqrt(var + LN_EPS) * lg_ref[...] + lb_ref[...]
    h_ref[...] = _silu(y).astype(BF16)
    bn_ref[:, 0:(taps - 2) * d] = buf_ref[:, d:(taps - 1) * d]
    bn_ref[:, (taps - 2) * d:(taps - 1) * d] = u


def cf_dec(a, dw_w, dw_b, ln_g, ln_b, buf):
    nb = a.shape[0]
    taps, d = dw_w.shape

    def full(shape, col=0):
        return pl.BlockSpec(shape, lambda i: (0, col))

    return pl.pallas_call(
        _cf_dec_kernel,
        grid=(1,),
        in_specs=[full((nb, d), 0), full((nb, d), 1), full((taps, d)), full((1, d)), full((1, d)), full((1, d)),
                  full((nb, (taps - 1) * d))],
        out_specs=[full((nb, d)), full((nb, (taps - 1) * d))],
        out_shape=[jax.ShapeDtypeStruct((nb, d), BF16), jax.ShapeDtypeStruct((nb, (taps - 1) * d), F32)],
        compiler_params=_cparams("arbitrary"),
        name="cf_dec",
    )(a, a, dw_w, dw_b, ln_g, ln_b, buf)


def kernel(x_prompt, x_sample, state_ssm, state_ssm_conv, state_sconv, state_cconv, meta_tokens,
           norm_mix, norm_ffn, norm_final, ssd_w_in, ssd_conv_w, ssd_conv_b, ssd_dt_bias, ssd_a_log,
           ssd_d, ssd_norm_w, ssd_w_out, sc_w_in, sc_conv_w, sc_w_out, cf_w_pw1, cf_b_pw1, cf_dw_w,
           cf_dw_b, cf_ln_g, cf_ln_b, cf_w_pw2, cf_b_pw2, ffn_w_gate, ffn_w_up, ffn_w_down):
    nb, seq, d = x_prompt.shape
    nd = x_sample.shape[0]
    assert x_sample.shape[1] == 1
    depth = norm_mix.shape[0]
    n_meta = meta_tokens.shape[0]
    heads = ssd_a_log.shape[1]
    d_inner = ssd_norm_w.shape[1]
    conv_dim = ssd_conv_w.shape[2]
    hp = heads * SSD_HEAD_DIM
    assert hp == d_inner and heads <= LANES and n_meta <= SSD_CHUNK
    assert seq % SSD_CHUNK == 0 and seq % LAYER_ROWS == 0
    ssd_n = d_inner + conv_dim + LANES

    x_main = x_prompt.reshape(nb * seq, d)
    x_meta = meta_tokens.astype(F32)
    x_dec = x_sample.reshape(nd, d)
    zeros_d = jnp.zeros((1, d), F32)
    states = state_ssm.reshape(state_ssm.shape[0], nd, hp, SSD_D_STATE)
    dec_states = None

    def row(v):
        return v.reshape(1, -1).astype(F32)

    def pad_lanes(v):
        return jnp.pad(v.reshape(1, -1).astype(F32), ((0, 0), (0, LANES - v.shape[-1])))

    p_ssm, p_ssm_conv, p_sconv, p_cconv = [], [], [], []
    s_ssm_conv, s_sconv, s_cconv = [], [], []

    w_gate_all, w_up_all, w_down_all = ffn_w_gate.astype(BF16), ffn_w_up.astype(BF16), ffn_w_down.astype(BF16)
    ssd_w_in_all = jnp.pad(ssd_w_in, ((0, 0), (0, 0), (0, ssd_n - ssd_w_in.shape[2]))).astype(BF16)
    ssd_w_out_all = ssd_w_out.astype(BF16)

    for i in range(depth):
        kind, j = i % 3, i // 3
        last = i == depth - 1
        g_mix = row(norm_mix[i])
        ffn_w = (row(norm_ffn[i]), _Stacked(w_gate_all, i), _Stacked(w_up_all, i), _Stacked(w_down_all, i),
                 row(norm_final))
        if kind == 0:
            w_in = _Stacked(ssd_w_in_all, j)
            conv_w, conv_b = ssd_conv_w[j], row(ssd_conv_b[j])
            dt_bias, a_log = pad_lanes(ssd_dt_bias[j]), pad_lanes(ssd_a_log[j])
            d_exp = jnp.repeat(ssd_d[j].astype(F32), SSD_HEAD_DIM).reshape(1, d_inner)
            norm_w = row(ssd_norm_w[j])
            w_out, b_out = _Stacked(ssd_w_out_all, j), zeros_d
            mix_w = (g_mix, w_in, conv_w, conv_b, dt_bias, a_log, d_exp, norm_w, w_out)
            zx_d = norm_matmul(x_dec, g_mix, w_in, jnp.zeros((1, ssd_n), F32), nd, _col_tile(ssd_n, 1024))
            cbuf = state_ssm_conv[j].reshape(nd, 3 * conv_dim)
            xs, xdt_t, bm, cm, da, cnew = ssd_dec_pre(zx_d, cbuf, conv_w, conv_b, dt_bias, a_log, d_inner)
            dec_states, y_d = ssd_dec_state(states, j, dec_states, xdt_t, bm, cm, da)
            h_d = ssd_dec_post(y_d, xs, zx_d, d_exp, norm_w)
            s_ssm_conv.append(cnew.reshape(nd, 3, conv_dim))
            x_pad = jnp.pad(x_meta, ((SSD_CHUNK - n_meta, 0), (0, 0)))
            x_pad, st_m, ct_m = ssd_layer(x_pad, mix_w, ffn_w, jnp.zeros((1, hp, SSD_D_STATE), F32),
                                          jnp.zeros((1, SUBLANES, conv_dim), F32), 1, SSD_CHUNK,
                                          SSD_CHUNK - n_meta, False)
            x_meta = x_pad[SSD_CHUNK - n_meta:]
            x_main, st_p, ct_p = ssd_layer(x_main, mix_w, ffn_w, st_m, ct_m, nb, seq, 0, last)
            p_ssm.append(st_p.reshape((nb,) + state_ssm.shape[2:]))
            p_ssm_conv.append(ct_p[:, SUBLANES - 3:, :])
        elif kind == 1:
            w_in = sc_w_in[j].astype(BF16)
            conv_w = sc_conv_w[j]
            w_out, b_out = sc_w_out[j].astype(BF16), zeros_d
            mix_w = (g_mix, w_in, conv_w, w_out)
            bcx_d = norm_matmul(x_dec, g_mix, w_in, jnp.zeros((1, 3 * d), F32), nd, 1024)
            h_d, bnew = sc_dec(bcx_d, conv_w, state_sconv[j].reshape(nd, 2 * d))
            s_sconv.append(bnew.reshape(nd, 2, d))
            x_meta, ct_m = sc_layer(x_meta, mix_w, ffn_w, jnp.zeros((1, SUBLANES, d), F32), 1, n_meta, n_meta, False)
            x_main, ct_p = sc_layer(x_main, mix_w, ffn_w, ct_m, nb, seq, LAYER_ROWS, last)
            p_sconv.append(ct_p[:, SUBLANES - 2:, :])
        else:
            w_in = cf_w_pw1[j].astype(BF16)
            dw_w = cf_dw_w[j]
            taps = dw_w.shape[0]
            w_out, b_out = cf_w_pw2[j].astype(BF16), row(cf_b_pw2[j])
            mix_w = (g_mix, w_in, row(cf_b_pw1[j]), dw_w, row(cf_dw_b[j]), row(cf_ln_g[j]), row(cf_ln_b[j]),
                     w_out, b_out)
            a_d = norm_matmul(x_dec, g_mix, w_in, row(cf_b_pw1[j]), nd, 1024)
            h_d, bnew = cf_dec(a_d, dw_w, row(cf_dw_b[j]), row(cf_ln_g[j]), row(cf_ln_b[j]),
                               state_cconv[j].reshape(nd, (taps - 1) * d))
            s_cconv.append(bnew.reshape(nd, taps - 1, d))
            x_meta, ct_m = cf_layer(x_meta, mix_w, ffn_w, jnp.zeros((1, CF_CARRY_ROWS, d), F32), 1, n_meta, n_meta,
                                    False)
            x_main, ct_p = cf_layer(x_main, mix_w, ffn_w, ct_m, nb, seq, LAYER_ROWS, last)
            p_cconv.append(ct_p[:, CF_CARRY_ROWS - (taps - 1):, :])
        x_dec = ffn(matmul_res(h_d, w_out, b_out, x_dec, nd), ffn_w, nd, last)

    y_prompt = x_main.reshape(nb, seq, d)
    y_sample = x_dec.reshape(nd, 1, d)
    s_ssm = dec_states.reshape(state_ssm.shape)
    return (y_prompt, y_sample, jnp.stack(p_ssm), jnp.stack(p_ssm_conv), jnp.stack(p_sconv), jnp.stack(p_cconv),
            s_ssm, jnp.stack(s_ssm_conv), jnp.stack(s_sconv), jnp.stack(s_cconv))
```

```python
import functools

import jax
import jax.numpy as jnp
from jax.experimental import pallas as pl
from jax.experimental.pallas import tpu as pltpu

F32 = jnp.float32
BF16 = jnp.bfloat16

EPS = 1e-6
LN_EPS = 1e-5
SSD_HEAD_DIM = 64
SSD_D_STATE = 128
SSD_GROUPS = 8
SSD_CHUNK = 256
SSD_SUBCHUNK = 128
HEADS_PER_GROUP = 4
GROUP_WIDTH = HEADS_PER_GROUP * SSD_HEAD_DIM
LANES = 128
SUBLANES = 8
MXU_WIDTH = 256
CF_CARRY_ROWS = 32
DEC_SEQ_BLOCK = 8
LAYER_ROWS = 512
VMEM_LIMIT_BYTES = 56 * 1024 * 1024
SSD_LAYER_VMEM_BYTES = 60 * 1024 * 1024


def _cparams(*sem, vmem=VMEM_LIMIT_BYTES):
    return pltpu.CompilerParams(dimension_semantics=sem, vmem_limit_bytes=vmem)


def _sigmoid(x):
    return 1.0 / (1.0 + jnp.exp(-x))


def _silu(x):
    return x * _sigmoid(x)


def _softplus(x):
    return jnp.maximum(x, 0.0) + jnp.log1p(jnp.exp(-jnp.abs(x)))


def _rms_rows(x, g):
    return x * jax.lax.rsqrt(jnp.mean(x * x, axis=-1, keepdims=True) + EPS) * g


def _col_tile(n, limit):
    return max(t for t in range(LANES, limit + 1, LANES) if n % t == 0)


def _ff_chunks(dff):
    step = 2 * MXU_WIDTH
    return tuple((lo, min(lo + step, dff)) for lo in range(0, dff, step))


class _Stacked:
    def __init__(self, array, layer):
        self.array, self.layer, self.shape = array, layer, array.shape[1:]


def _operand(w):
    return w.array if isinstance(w, _Stacked) else w


def _resident(w):
    if isinstance(w, _Stacked):
        return pl.BlockSpec((None,) + w.shape, lambda *_: (w.layer,) + (0,) * len(w.shape),
                            pipeline_mode=pl.Buffered(1))
    return pl.BlockSpec(w.shape, lambda *_: (0,) * len(w.shape), pipeline_mode=pl.Buffered(1))


def _norm_matmul_kernel(x_ref, g_ref, w_ref, b_ref, o_ref, hn_ref):
    @pl.when(pl.program_id(1) == 0)
    def _():
        hn_ref[...] = _rms_rows(x_ref[...], g_ref[...]).astype(BF16)

    o_ref[...] = jnp.dot(hn_ref[...], w_ref[...], preferred_element_type=F32) + b_ref[...]


def norm_matmul(x, g, w, b, tm, tn):
    m, d = x.shape
    n = w.shape[1]
    if isinstance(w, _Stacked):
        w_spec = pl.BlockSpec((None, d, tn), lambda i, j: (w.layer, 0, j))
    else:
        w_spec = pl.BlockSpec((d, tn), lambda i, j: (0, j))
    return pl.pallas_call(
        _norm_matmul_kernel,
        grid=(m // tm, n // tn),
        in_specs=[
            pl.BlockSpec((tm, d), lambda i, j: (i, 0)),
            pl.BlockSpec((1, d), lambda i, j: (0, 0)),
            w_spec,
            pl.BlockSpec((1, tn), lambda i, j: (0, j)),
        ],
        out_specs=pl.BlockSpec((tm, tn), lambda i, j: (i, j)),
        out_shape=jax.ShapeDtypeStruct((m, n), F32),
        scratch_shapes=[pltpu.VMEM((tm, d), BF16)],
        compiler_params=_cparams("parallel", "arbitrary"),
        name="norm_matmul",
    )(x, g, _operand(w), b)


def _matmul_res_kernel(h_ref, w_ref, b_ref, x_ref, o_ref):
    o_ref[...] = x_ref[...] + b_ref[...] + jnp.dot(h_ref[...], w_ref[...], preferred_element_type=F32)


def matmul_res(h, w, b, x, tm):
    m, k = h.shape
    d = w.shape[1]
    return pl.pallas_call(
        _matmul_res_kernel,
        grid=(m // tm,),
        in_specs=[
            pl.BlockSpec((tm, k), lambda i: (i, 0)),
            _resident(w),
            _resident(b),
            pl.BlockSpec((tm, d), lambda i: (i, 0)),
        ],
        out_specs=pl.BlockSpec((tm, d), lambda i: (i, 0)),
        out_shape=jax.ShapeDtypeStruct((m, d), F32),
        compiler_params=_cparams("parallel"),
        name="matmul_res",
    )(h, _operand(w), b, x)


def _ffn_steps(x_ref, g_ref, wg_ref, wu_ref, wd_ref, gf_ref, o_ref, acc_ref, final_norm):
    held = {}

    def start():
        x = x_ref[...]
        held["hn"] = _rms_rows(x, g_ref[...]).astype(BF16)
        acc_ref[...] = x

    def chunk(lo, hi):
        def gate():
            held["gate"] = jnp.dot(held["hn"], wg_ref[:, lo:hi], preferred_element_type=F32)

        def up():
            up_proj = jnp.dot(held["hn"], wu_ref[:, lo:hi], preferred_element_type=F32)
            held["act"] = (_silu(held["gate"]) * up_proj).astype(BF16)

        def down():
            acc_ref[...] += jnp.dot(held["act"], wd_ref[lo:hi, :], preferred_element_type=F32)
        return [gate, up, down]

    def finish():
        y = acc_ref[...]
        if final_norm:
            y = _rms_rows(y, gf_ref[...])
        o_ref[...] = y

    return [start] + [step for lo, hi in _ff_chunks(wg_ref.shape[1]) for step in chunk(lo, hi)] + [finish]


class _Drip:
    def __init__(self, steps, slots):
        self.steps, self.slots, self.ticks, self.done = steps, slots, 0, 0

    def tick(self):
        self.ticks += 1
        self._run_to(self.ticks * len(self.steps) // self.slots)

    def flush(self):
        self._run_to(len(self.steps))

    def _run_to(self, target):
        while self.done < min(target, len(self.steps)):
            self.steps[self.done]()
            self.done += 1


def _ffn_kernel(x_ref, g_ref, wg_ref, wu_ref, wd_ref, gf_ref, o_ref, acc_ref, *, final_norm):
    for step in _ffn_steps(x_ref, g_ref, wg_ref, wu_ref, wd_ref, gf_ref, o_ref, acc_ref, final_norm):
        step()


def _resident_all(ws):
    return [_resident(w) for w in ws]


def _operands(ws):
    return [_operand(w) for w in ws]


def ffn(x, ffn_w, tm, final_norm):
    m, d = x.shape
    dff = ffn_w[1].shape[1]
    return pl.pallas_call(
        functools.partial(_ffn_kernel, final_norm=final_norm),
        grid=(m // tm,),
        in_specs=[pl.BlockSpec((tm, d), lambda i: (i, 0))] + _resident_all(ffn_w),
        out_specs=pl.BlockSpec((tm, d), lambda i: (i, 0)),
        out_shape=jax.ShapeDtypeStruct((m, d), F32),
        scratch_shapes=[pltpu.VMEM((tm, d), F32)],
        compiler_params=_cparams("parallel"),
        name="ffn",
    )(x, *_operands(ffn_w))


def _lag_specs(tiles, rows, d):
    x_spec = pl.BlockSpec((rows, d), lambda t: (jnp.minimum(t, tiles - 1), 0))
    o_spec = pl.BlockSpec((rows, d), lambda t: (jnp.maximum(t - 1, 0), 0))
    return x_spec, o_spec


def _seq_out_spec(shape, tiles, per_seq):
    return pl.BlockSpec((1,) + shape, lambda t: (jnp.minimum(t, tiles - 1) // per_seq, 0, 0))


def _seq_init_spec(shape, shared, tiles, per_seq):
    if shared:
        return pl.BlockSpec((1,) + shape, lambda t: (0, 0, 0), pipeline_mode=pl.Buffered(1))
    return _seq_out_spec(shape, tiles, per_seq)


def _cumsum_rows(x):
    rows = x.shape[0]
    row = jax.lax.broadcasted_iota(jnp.int32, x.shape, 0)
    shift = 1
    while shift < rows:
        x = x + jnp.where(row >= shift, pltpu.roll(x, shift, axis=0), 0.0)
        shift *= 2
    return x


def _expand_heads4(cols, lane):
    out = cols[:, 3:4]
    for j in (2, 1, 0):
        out = jnp.where(lane < (j + 1) * SSD_HEAD_DIM, cols[:, j:j + 1], out)
    return out


def _ssd_layer_kernel(x_ref, gm_ref, win_ref, cw_ref, cb_ref, dtb_ref, alog_ref, dexp_ref, nw_ref, wout_ref,
                      s0_ref, c0_ref, gf_ref, wg_ref, wu_ref, wd_ref, gl_ref,
                      o_ref, sn_ref, cn_ref,
                      xmid_ref, acc_ref, st_ref, xpad_ref, z_ref, y_ref,
                      *, rows, sub, n_pad, tiles, per_seq, final_norm):
    t = pl.program_id(0)
    c = jax.lax.rem(t, per_seq)
    d_inner = z_ref.shape[1]
    conv_dim = xpad_ref.shape[1]
    n = SSD_D_STATE
    gw = GROUP_WIDTH
    pw = 2 * MXU_WIDTH

    @pl.when(t == 0)
    def _():
        xmid_ref[...] = jnp.zeros_like(xmid_ref)

    @pl.when(c == 0)
    def _():
        for g in range(SSD_GROUPS):
            st_ref[g] = s0_ref[0, g * gw:(g + 1) * gw, :].T
        xpad_ref[0:SUBLANES, :] = c0_ref[0]

    conv_tick = 4
    drip = _Drip(_ffn_steps(xmid_ref, gf_ref, wg_ref, wu_ref, wd_ref, gl_ref, o_ref, acc_ref, final_norm),
                 conv_dim // LANES // conv_tick + (rows // sub) * SSD_GROUPS)

    x = x_ref[...]
    hn = _rms_rows(x, gm_ref[...]).astype(BF16)
    for s in range(d_inner // pw):
        z_ref[:, s * pw:(s + 1) * pw] = jnp.dot(hn, win_ref[:, s * pw:(s + 1) * pw], preferred_element_type=F32)
    for s in range(conv_dim // pw):
        xpad_ref[SUBLANES:SUBLANES + rows, s * pw:(s + 1) * pw] = jnp.dot(
            hn, win_ref[:, d_inner + s * pw:d_inner + (s + 1) * pw], preferred_element_type=F32)
    dt_raw = jnp.dot(hn, win_ref[:, d_inner + conv_dim:d_inner + conv_dim + LANES], preferred_element_type=F32)

    carry = xpad_ref[rows:rows + SUBLANES, :]
    rb = min(rows, 128)
    for s in range(conv_dim // LANES):
        sl = slice(s * LANES, (s + 1) * LANES)
        for r0 in reversed(range(0, rows, rb)):
            base = xpad_ref[r0:r0 + rb + SUBLANES, sl]
            acc = cb_ref[:, sl] + cw_ref[3:4, sl] * base[SUBLANES:SUBLANES + rb, :]
            for k in range(3):
                win = pltpu.roll(base, rb + 3 - k, axis=0)
                acc = acc + cw_ref[k:k + 1, sl] * win[0:rb, :]
            xpad_ref[SUBLANES + r0:SUBLANES + r0 + rb, sl] = _silu(acc)
        if s % conv_tick == conv_tick - 1:
            drip.tick()
    xpad_ref[0:SUBLANES, :] = carry

    dt_all = _softplus(dt_raw + dtb_ref[...])
    if n_pad:
        prow = jax.lax.broadcasted_iota(jnp.int32, dt_all.shape, 0)
        dt_all = jnp.where(prow >= n_pad, dt_all, 0.0)
    a = -jnp.exp(alog_ref[...])

    ti = jax.lax.broadcasted_iota(jnp.int32, (sub, sub), 0)
    si = jax.lax.broadcasted_iota(jnp.int32, (sub, sub), 1)
    causal = ti >= si
    lane_g = jax.lax.broadcasted_iota(jnp.int32, (sub, gw), 1)
    lane_1 = jax.lax.broadcasted_iota(jnp.int32, (1, gw), 1)

    def block_decays(u):
        dt = dt_all[u * sub:(u + 1) * sub, :]
        cs = _cumsum_rows(dt * a)
        cs_last = cs[sub - 1:sub, :]
        return (cs, cs.T, dt.T,
                jnp.exp(cs),
                dt * jnp.exp(cs_last - cs),
                jnp.exp(cs_last))

    def group_bc(u, g):
        rs = slice(SUBLANES + u * sub, SUBLANES + (u + 1) * sub)
        bg_t = xpad_ref[rs, d_inner + g * n:d_inner + (g + 1) * n].T.astype(BF16)
        c_lo = d_inner + SSD_GROUPS * n + g * n
        cg_b = xpad_ref[rs, c_lo:c_lo + n].astype(BF16)
        return bg_t, cg_b, jnp.dot(cg_b, bg_t, preferred_element_type=F32)

    blocks = range(rows // sub)
    decays = [block_decays(u) for u in blocks]
    bcs = [[group_bc(u, g) for g in range(SSD_GROUPS)] for u in blocks]
    for u in blocks:
        rs = slice(SUBLANES + u * sub, SUBLANES + (u + 1) * sub)
        us = slice(u * sub, (u + 1) * sub)
        cs, cs_t, dt_t, e_cs, w_end, e_last = decays[u]
        for g in range(SSD_GROUPS):
            gsl = slice(g * gw, (g + 1) * gw)
            hsl = slice(g * HEADS_PER_GROUP, (g + 1) * HEADS_PER_GROUP)
            xg = xpad_ref[rs, gsl]
            xg_b = xg.astype(BF16)
            bg_t, cg_b, cb = bcs[u][g]
            st = st_ref[g]
            y = jnp.dot(cg_b, st.astype(BF16), preferred_element_type=F32) * _expand_heads4(e_cs[:, hsl], lane_g)
            ms, xms = [], []
            for j in range(HEADS_PER_GROUP):
                h = g * HEADS_PER_GROUP + j
                seg = cs[:, h:h + 1] - cs_t[h:h + 1, :]
                m = cb * jnp.exp(jnp.where(causal, seg, -jnp.inf)) * dt_t[h:h + 1, :]
                in_head = (lane_g >= j * SSD_HEAD_DIM) & (lane_g < (j + 1) * SSD_HEAD_DIM)
                ms.append(m.astype(BF16))
                xms.append(jnp.where(in_head, xg_b, jnp.zeros_like(xg_b)))
            y = y + jnp.dot(jnp.concatenate(ms, axis=1), jnp.concatenate(xms, axis=0), preferred_element_type=F32)
            y = y + dexp_ref[:, gsl] * xg
            wg = (xg * _expand_heads4(w_end[:, hsl], lane_g)).astype(BF16)
            st_ref[g] = st * _expand_heads4(e_last[:, hsl], lane_1) + jnp.dot(bg_t, wg, preferred_element_type=F32)
            y = y * _silu(z_ref[us, gsl])
            y_ref[us, gsl] = _rms_rows(y, nw_ref[:, gsl]).astype(BF16)
            drip.tick()

    drip.flush()
    xmid_ref[...] = x + jnp.dot(y_ref[...], wout_ref[...], preferred_element_type=F32)

    @pl.when((c == per_seq - 1) & (t < tiles))
    def _():
        for g in range(SSD_GROUPS):
            sn_ref[0, g * gw:(g + 1) * gw, :] = st_ref[g].T
        cn_ref[0] = carry


def ssd_layer(x, mix_w, ffn_w, s0, c0, nb, seq, n_pad, final_norm):
    rows = SSD_CHUNK
    d = x.shape[1]
    w_in, conv_w, norm_w, w_out = mix_w[1], mix_w[2], mix_w[7], mix_w[8]
    d_inner = norm_w.shape[1]
    conv_dim = conv_w.shape[1]
    dff = ffn_w[1].shape[1]
    hp = s0.shape[1]
    per_seq = seq // rows
    tiles = nb * per_seq
    shared0 = s0.shape[0] == 1
    assert w_in.shape[1] == d_inner + conv_dim + LANES
    x_spec, o_spec = _lag_specs(tiles, rows, d)
    return pl.pallas_call(
        functools.partial(_ssd_layer_kernel, rows=rows, sub=SSD_SUBCHUNK, n_pad=n_pad, tiles=tiles, per_seq=per_seq,
                          final_norm=final_norm),
        grid=(tiles + 1,),
        in_specs=[x_spec] + _resident_all(mix_w) + [
            _seq_init_spec((hp, SSD_D_STATE), shared0, tiles, per_seq),
            _seq_init_spec((SUBLANES, conv_dim), shared0, tiles, per_seq),
        ] + _resident_all(ffn_w),
        out_specs=[
            o_spec,
            _seq_out_spec((hp, SSD_D_STATE), tiles, per_seq),
            _seq_out_spec((SUBLANES, conv_dim), tiles, per_seq),
        ],
        out_shape=[
            jax.ShapeDtypeStruct((nb * seq, d), F32),
            jax.ShapeDtypeStruct((nb, hp, SSD_D_STATE), F32),
            jax.ShapeDtypeStruct((nb, SUBLANES, conv_dim), F32),
        ],
        scratch_shapes=[
            pltpu.VMEM((rows, d), F32),
            pltpu.VMEM((rows, d), F32),
            pltpu.VMEM((SSD_GROUPS, SSD_D_STATE, GROUP_WIDTH), F32),
            pltpu.VMEM((rows + SUBLANES, conv_dim), F32),
            pltpu.VMEM((rows, d_inner), F32),
            pltpu.VMEM((rows, d_inner), BF16),
        ],
        compiler_params=_cparams("arbitrary", vmem=SSD_LAYER_VMEM_BYTES),
        name="ssd_layer",
    )(x, *_operands(mix_w), s0, c0, *_operands(ffn_w))


def _sc_layer_kernel(x_ref, gm_ref, win_ref, cw_ref, wout_ref, c0_ref, gf_ref, wg_ref, wu_ref, wd_ref, gl_ref,
                     o_ref, cn_ref, xmid_ref, acc_ref, ppad_ref, *, rows, tiles, per_seq, final_norm):
    t = pl.program_id(0)
    c = jax.lax.rem(t, per_seq)

    @pl.when(t == 0)
    def _():
        xmid_ref[...] = jnp.zeros_like(xmid_ref)

    @pl.when(c == 0)
    def _():
        ppad_ref[0:SUBLANES, :] = c0_ref[0]

    for step in _ffn_steps(xmid_ref, gf_ref, wg_ref, wu_ref, wd_ref, gl_ref, o_ref, acc_ref, final_norm):
        step()

    x = x_ref[...]
    d = x.shape[1]
    hn = _rms_rows(x, gm_ref[...]).astype(BF16)
    p = (jnp.dot(hn, win_ref[:, d:2 * d], preferred_element_type=F32)
         * jnp.dot(hn, win_ref[:, 2 * d:3 * d], preferred_element_type=F32))
    ppad_ref[SUBLANES:SUBLANES + rows, :] = p
    v = (cw_ref[0:1, :] * ppad_ref[SUBLANES - 2:SUBLANES - 2 + rows, :]
         + cw_ref[1:2, :] * ppad_ref[SUBLANES - 1:SUBLANES - 1 + rows, :]
         + cw_ref[2:3, :] * p)
    h = (jnp.dot(hn, win_ref[:, 0:d], preferred_element_type=F32) * v).astype(BF16)
    xmid_ref[...] = x + jnp.dot(h, wout_ref[...], preferred_element_type=F32)
    carry = ppad_ref[rows:rows + SUBLANES, :]
    ppad_ref[0:SUBLANES, :] = carry

    @pl.when((c == per_seq - 1) & (t < tiles))
    def _():
        cn_ref[0] = carry


def sc_layer(x, mix_w, ffn_w, c0, nb, seq, rows, final_norm):
    d = x.shape[1]
    dff = ffn_w[1].shape[1]
    per_seq = seq // rows
    tiles = nb * per_seq
    x_spec, o_spec = _lag_specs(tiles, rows, d)
    return pl.pallas_call(
        functools.partial(_sc_layer_kernel, rows=rows, tiles=tiles, per_seq=per_seq, final_norm=final_norm),
        grid=(tiles + 1,),
        in_specs=[x_spec] + _resident_all(mix_w) + [
            _seq_init_spec((SUBLANES, d), c0.shape[0] == 1, tiles, per_seq),
        ] + _resident_all(ffn_w),
        out_specs=[o_spec, _seq_out_spec((SUBLANES, d), tiles, per_seq)],
        out_shape=[jax.ShapeDtypeStruct((nb * seq, d), F32), jax.ShapeDtypeStruct((nb, SUBLANES, d), F32)],
        scratch_shapes=[pltpu.VMEM((rows, d), F32), pltpu.VMEM((rows, d), F32), pltpu.VMEM((rows + SUBLANES, d), F32)],
        compiler_params=_cparams("arbitrary"),
        name="sc_layer",
    )(x, *_operands(mix_w), c0, *_operands(ffn_w))


def _cf_layer_kernel(x_ref, gm_ref, w1_ref, b1_ref, dw_ref, db_ref, lg_ref, lb_ref, w2_ref, b2_ref, c0_ref,
                     gf_ref, wg_ref, wu_ref, wd_ref, gl_ref,
                     o_ref, cn_ref, xmid_ref, acc_ref, upad_ref, v_ref, *, rows, tiles, per_seq, final_norm):
    t = pl.program_id(0)
    c = jax.lax.rem(t, per_seq)
    taps = dw_ref.shape[0]
    hist = CF_CARRY_ROWS
    first = hist - (taps - 1)

    @pl.when(t == 0)
    def _():
        xmid_ref[...] = jnp.zeros_like(xmid_ref)

    @pl.when(c == 0)
    def _():
        upad_ref[0:hist, :] = c0_ref[0]

    x = x_ref[...]
    d = x.shape[1]
    rb = min(rows, 128)
    drip = _Drip(_ffn_steps(xmid_ref, gf_ref, wg_ref, wu_ref, wd_ref, gl_ref, o_ref, acc_ref, final_norm),
                 (d // LANES) * (rows // rb))
    hn = _rms_rows(x, gm_ref[...]).astype(BF16)
    u = ((jnp.dot(hn, w1_ref[:, 0:d], preferred_element_type=F32) + b1_ref[:, 0:d])
         * _sigmoid(jnp.dot(hn, w1_ref[:, d:2 * d], preferred_element_type=F32) + b1_ref[:, d:2 * d]))
    upad_ref[hist:hist + rows, :] = u
    for s in range(d // LANES):
        sl = slice(s * LANES, (s + 1) * LANES)
        for r0 in range(0, rows, rb):
            base = upad_ref[r0:r0 + rb + hist, sl]
            acc = jnp.broadcast_to(db_ref[:, sl], (rb, LANES))
            for phase in range(SUBLANES):
                steps = [q for q in range(hist // SUBLANES + 1) if first <= q * SUBLANES + phase <= hist]
                win = base if phase == 0 else pltpu.roll(base, rb + hist - phase, axis=0)
                for q in steps:
                    k = q * SUBLANES + phase - first
                    acc = acc + dw_ref[k:k + 1, sl] * win[q * SUBLANES:q * SUBLANES + rb, :]
            v_ref[r0:r0 + rb, sl] = acc
            drip.tick()
    drip.flush()
    v = v_ref[...]
    mu = jnp.mean(v, axis=-1, keepdims=True)
    dv = v - mu
    var = jnp.mean(dv * dv, axis=-1, keepdims=True)
    y = dv * jax.lax.rsqrt(var + LN_EPS) * lg_ref[...] + lb_ref[...]
    h = _silu(y).astype(BF16)
    xmid_ref[...] = x + b2_ref[...] + jnp.dot(h, w2_ref[...], preferred_element_type=F32)
    carry = upad_ref[rows:rows + hist, :]
    upad_ref[0:hist, :] = carry

    @pl.when((c == per_seq - 1) & (t < tiles))
    def _():
        cn_ref[0] = carry


def cf_layer(x, mix_w, ffn_w, c0, nb, seq, rows, final_norm):
    d = x.shape[1]
    dff = ffn_w[1].shape[1]
    per_seq = seq // rows
    tiles = nb * per_seq
    assert mix_w[3].shape[0] - 1 <= CF_CARRY_ROWS
    x_spec, o_spec = _lag_specs(tiles, rows, d)
    return pl.pallas_call(
        functools.partial(_cf_layer_kernel, rows=rows, tiles=tiles, per_seq=per_seq, final_norm=final_norm),
        grid=(tiles + 1,),
        in_specs=[x_spec] + _resident_all(mix_w) + [
            _seq_init_spec((CF_CARRY_ROWS, d), c0.shape[0] == 1, tiles, per_seq),
        ] + _resident_all(ffn_w),
        out_specs=[o_spec, _seq_out_spec((CF_CARRY_ROWS, d), tiles, per_seq)],
        out_shape=[jax.ShapeDtypeStruct((nb * seq, d), F32), jax.ShapeDtypeStruct((nb, CF_CARRY_ROWS, d), F32)],
        scratch_shapes=[pltpu.VMEM((rows, d), F32), pltpu.VMEM((rows, d), F32),
                        pltpu.VMEM((rows + CF_CARRY_ROWS, d), F32), pltpu.VMEM((rows, d), F32)],
        compiler_params=_cparams("arbitrary"),
        name="cf_layer",
    )(x, *_operands(mix_w), c0, *_operands(ffn_w))


def _expand_heads_mxu(v, d_inner):
    hrow = jax.lax.broadcasted_iota(jnp.int32, (LANES, d_inner), 0)
    col = jax.lax.broadcasted_iota(jnp.int32, (LANES, d_inner), 1)
    onehot = jnp.where((col >= hrow * SSD_HEAD_DIM) & (col < (hrow + 1) * SSD_HEAD_DIM), 1.0, 0.0).astype(BF16)
    hi = v.astype(BF16)
    r1 = v - hi.astype(F32)
    mid = r1.astype(BF16)
    lo = (r1 - mid.astype(F32)).astype(BF16)
    out = jnp.dot(hi, onehot, preferred_element_type=F32)
    out = out + jnp.dot(mid, onehot, preferred_element_type=F32)
    return out + jnp.dot(lo, onehot, preferred_element_type=F32)


def _ssd_dec_pre_kernel(xa_ref, xb_ref, dt_ref, cbuf_ref, cw_ref, cb_ref, dtb_ref, alog_ref,
                        xs_ref, xdt_t_ref, b_ref, c_ref, da_ref, cnew_ref):
    conv_dim = cw_ref.shape[1]
    d_inner = xs_ref.shape[1]
    gn = b_ref.shape[1]

    def conv_silu(xnew, lo, hi):
        acc = cb_ref[:, lo:hi] + cw_ref[3:4, lo:hi] * xnew
        for k in range(3):
            acc = acc + cw_ref[k:k + 1, lo:hi] * cbuf_ref[:, k * conv_dim + lo:k * conv_dim + hi]
        return _silu(acc)

    xa = xa_ref[...]
    xb = xb_ref[...]
    cnew_ref[:, 0:2 * conv_dim] = cbuf_ref[:, conv_dim:3 * conv_dim]
    cnew_ref[:, 2 * conv_dim:2 * conv_dim + d_inner] = xa
    cnew_ref[:, 2 * conv_dim + d_inner:3 * conv_dim] = xb
    dt = _softplus(dt_ref[...] + dtb_ref[...])
    da_ref[...] = jnp.exp(dt * -jnp.exp(alog_ref[...]))
    xs = conv_silu(xa, 0, d_inner)
    xs_ref[...] = xs
    xdt_t_ref[...] = (xs * _expand_heads_mxu(dt, d_inner)).T.astype(BF16)
    bc = conv_silu(xb, d_inner, conv_dim)
    b_ref[...] = bc[:, 0:gn]
    c_ref[...] = bc[:, gn:2 * gn]


def ssd_dec_pre(zx, cbuf, conv_w, conv_b, dt_bias, a_log, d_inner):
    nb = zx.shape[0]
    conv_dim = conv_w.shape[1]
    assert conv_dim == 2 * d_inner
    gn = (conv_dim - d_inner) // 2
    dt_blk = (d_inner + conv_dim) // LANES

    def full(shape):
        return pl.BlockSpec(shape, lambda i: (0, 0))

    return pl.pallas_call(
        _ssd_dec_pre_kernel,
        grid=(1,),
        in_specs=[
            pl.BlockSpec((nb, d_inner), lambda i: (0, 1)),
            pl.BlockSpec((nb, d_inner), lambda i: (0, 2)),
            pl.BlockSpec((nb, LANES), lambda i: (0, dt_blk)),
            full((nb, 3 * conv_dim)), full((4, conv_dim)), full((1, conv_dim)), full((1, LANES)), full((1, LANES)),
        ],
        out_specs=[
            full((nb, d_inner)), full((d_inner, nb)), full((nb, gn)), full((nb, gn)), full((nb, LANES)),
            full((nb, 3 * conv_dim)),
        ],
        out_shape=[
            jax.ShapeDtypeStruct((nb, d_inner), F32), jax.ShapeDtypeStruct((d_inner, nb), BF16),
            jax.ShapeDtypeStruct((nb, gn), F32), jax.ShapeDtypeStruct((nb, gn), F32),
            jax.ShapeDtypeStruct((nb, LANES), F32), jax.ShapeDtypeStruct((nb, 3 * conv_dim), F32),
        ],
        compiler_params=_cparams("arbitrary"),
        name="ssd_dec_pre",
    )(zx, zx, zx, cbuf, conv_w, conv_b, dt_bias, a_log)


def _ssd_dec_state_kernel(*refs, chained):
    if chained:
        refs = refs[1:]
    s_ref, xdt_t_ref, b_ref, c_ref, da_ref, sn_ref, y_ref = refs
    i = pl.program_id(0)
    bs = s_ref.shape[0]
    hp, nb = xdt_t_ref.shape
    n = SSD_D_STATE
    gw = GROUP_WIDTH
    groups = hp // gw
    c_blk = c_ref[...].astype(BF16)
    seq_row = jax.lax.broadcasted_iota(jnp.int32, (nb, n), 0)
    out_row = jax.lax.broadcasted_iota(jnp.int32, (bs, gw), 0)

    def one_seq(k, ys):
        da_row = da_ref[pl.ds(k, 1), :]
        new_ys = []
        for g in range(groups):
            b_sel = jnp.where(seq_row == i * bs + k, b_ref[:, g * n:(g + 1) * n], 0.0).astype(BF16)
            outer = jnp.dot(xdt_t_ref[g * gw:(g + 1) * gw, :], b_sel, preferred_element_type=F32)
            parts = []
            for j in range(HEADS_PER_GROUP):
                h = g * HEADS_PER_GROUP + j
                hsl = slice(g * gw + j * SSD_HEAD_DIM, g * gw + (j + 1) * SSD_HEAD_DIM)
                snew = s_ref[k, hsl, :] * da_row[:, h:h + 1] + outer[j * SSD_HEAD_DIM:(j + 1) * SSD_HEAD_DIM, :]
                sn_ref[k, hsl, :] = snew
                parts.append(snew.astype(BF16))
            y8 = jax.lax.dot_general(c_blk[:, g * n:(g + 1) * n], jnp.concatenate(parts, axis=0),
                                     (((1,), (1,)), ((), ())), preferred_element_type=F32)
            new_ys.append(jnp.where(out_row == k, y8, ys[g]))
        return tuple(new_ys)

    ys = jax.lax.fori_loop(0, bs, one_seq, tuple(jnp.zeros((bs, gw), F32) for _ in range(groups)))
    for g in range(groups):
        y_ref[:, g * gw:(g + 1) * gw] = ys[g]


def ssd_dec_state(states, layer, prev, xdt_t, bm, cm, da):
    n_layers, nb, hp, n = states.shape
    bs = DEC_SEQ_BLOCK
    chained = prev is not None

    def full(shape):
        return pl.BlockSpec(shape, lambda i: (0, 0))

    state_blk = pl.BlockSpec((None, bs, hp, n), lambda i: (layer, i, 0, 0))
    in_specs = [state_blk, full((hp, nb)), full(bm.shape),
                pl.BlockSpec((bs, cm.shape[1]), lambda i: (i, 0)), pl.BlockSpec((bs, LANES), lambda i: (i, 0))]
    args = [states, xdt_t, bm, cm, da]
    if chained:
        in_specs = [pl.BlockSpec(memory_space=pl.ANY)] + in_specs
        args = [prev] + args
    return pl.pallas_call(
        functools.partial(_ssd_dec_state_kernel, chained=chained),
        grid=(nb // bs,),
        in_specs=in_specs,
        out_specs=[state_blk, pl.BlockSpec((bs, hp), lambda i: (i, 0))],
        out_shape=[jax.ShapeDtypeStruct(states.shape, F32), jax.ShapeDtypeStruct((nb, hp), F32)],
        input_output_aliases={0: 0} if chained else {},
        compiler_params=_cparams("arbitrary"),
        name="ssd_dec_state",
    )(*args)


def _ssd_dec_post_kernel(y_ref, xs_ref, z_ref, dexp_ref, nw_ref, h_ref):
    gw = GROUP_WIDTH
    y = (y_ref[...] + dexp_ref[...] * xs_ref[...]) * _silu(z_ref[...])
    for g in range(y.shape[1] // gw):
        gsl = slice(g * gw, (g + 1) * gw)
        h_ref[:, gsl] = _rms_rows(y[:, gsl], nw_ref[:, gsl]).astype(BF16)


def ssd_dec_post(y, xs, zx, d_exp, norm_w):
    nb, d_inner = y.shape

    def full(shape):
        return pl.BlockSpec(shape, lambda i: (0, 0))

    return pl.pallas_call(
        _ssd_dec_post_kernel,
        grid=(1,),
        in_specs=[full((nb, d_inner)), full((nb, d_inner)), full((nb, d_inner)), full((1, d_inner)), full((1, d_inner))],
        out_specs=full((nb, d_inner)),
        out_shape=jax.ShapeDtypeStruct((nb, d_inner), BF16),
        compiler_params=_cparams("arbitrary"),
        name="ssd_dec_post",
    )(y, xs, zx, d_exp, norm_w)


def _sc_dec_kernel(bg_ref, cg_ref, xv_ref, w_ref, buf_ref, h_ref, bn_ref):
    d = w_ref.shape[1]
    p = cg_ref[...] * xv_ref[...]
    v = w_ref[0:1, :] * buf_ref[:, 0:d] + w_ref[1:2, :] * buf_ref[:, d:2 * d] + w_ref[2:3, :] * p
    h_ref[...] = (bg_ref[...] * v).astype(BF16)
    bn_ref[:, 0:d] = buf_ref[:, d:2 * d]
    bn_ref[:, d:2 * d] = p


def sc_dec(bcx, conv_w, buf):
    nb = bcx.shape[0]
    d = conv_w.shape[1]

    def full(shape, col=0):
        return pl.BlockSpec(shape, lambda i: (0, col))

    return pl.pallas_call(
        _sc_dec_kernel,
        grid=(1,),
        in_specs=[full((nb, d), 0), full((nb, d), 1), full((nb, d), 2), full((3, d)), full((nb, 2 * d))],
        out_specs=[full((nb, d)), full((nb, 2 * d))],
        out_shape=[jax.ShapeDtypeStruct((nb, d), BF16), jax.ShapeDtypeStruct((nb, 2 * d), F32)],
        compiler_params=_cparams("arbitrary"),
        name="sc_dec",
    )(bcx, bcx, bcx, conv_w, buf)


def _cf_dec_kernel(a1_ref, a2_ref, w_ref, b_ref, lg_ref, lb_ref, buf_ref, h_ref, bn_ref):
    taps, d = w_ref.shape
    u = a1_ref[...] * _sigmoid(a2_ref[...])
    v = b_ref[...] + w_ref[taps - 1:taps, :] * u
    for k in range(taps - 1):
        v = v + w_ref[k:k + 1, :] * buf_ref[:, k * d:(k + 1) * d]
    mu = jnp.mean(v, axis=-1, keepdims=True)
    dv = v - mu
    var = jnp.mean(dv * dv, axis=-1, keepdims=True)
    y = dv * jax.lax.rsqrt(var + LN_EPS) * lg_ref[...] + lb_ref[...]
    h_ref[...] = _silu(y).astype(BF16)
    bn_ref[:, 0:(taps - 2) * d] = buf_ref[:, d:(taps - 1) * d]
    bn_ref[:, (taps - 2) * d:(taps - 1) * d] = u


def cf_dec(a, dw_w, dw_b, ln_g, ln_b, buf):
    nb = a.shape[0]
    taps, d = dw_w.shape

    def full(shape, col=0):
        return pl.BlockSpec(shape, lambda i: (0, col))

    return pl.pallas_call(
        _cf_dec_kernel,
        grid=(1,),
        in_specs=[full((nb, d), 0), full((nb, d), 1), full((taps, d)), full((1, d)), full((1, d)), full((1, d)),
                  full((nb, (taps - 1) * d))],
        out_specs=[full((nb, d)), full((nb, (taps - 1) * d))],
        out_shape=[jax.ShapeDtypeStruct((nb, d), BF16), jax.ShapeDtypeStruct((nb, (taps - 1) * d), F32)],
        compiler_params=_cparams("arbitrary"),
        name="cf_dec",
    )(a, a, dw_w, dw_b, ln_g, ln_b, buf)


def kernel(x_prompt, x_sample, state_ssm, state_ssm_conv, state_sconv, state_cconv, meta_tokens,
           norm_mix, norm_ffn, norm_final, ssd_w_in, ssd_conv_w, ssd_conv_b, ssd_dt_bias, ssd_a_log,
           ssd_d, ssd_norm_w, ssd_w_out, sc_w_in, sc_conv_w, sc_w_out, cf_w_pw1, cf_b_pw1, cf_dw_w,
           cf_dw_b, cf_ln_g, cf_ln_b, cf_w_pw2, cf_b_pw2, ffn_w_gate, ffn_w_up, ffn_w_down):
    nb, seq, d = x_prompt.shape
    nd = x_sample.shape[0]
    assert x_sample.shape[1] == 1
    depth = norm_mix.shape[0]
    n_meta = meta_tokens.shape[0]
    heads = ssd_a_log.shape[1]
    d_inner = ssd_norm_w.shape[1]
    conv_dim = ssd_conv_w.shape[2]
    hp = heads * SSD_HEAD_DIM
    assert hp == d_inner and heads <= LANES and n_meta <= SSD_CHUNK
    assert seq % SSD_CHUNK == 0 and seq % LAYER_ROWS == 0
    ssd_n = d_inner + conv_dim + LANES

    x_main = x_prompt.reshape(nb * seq, d)
    x_meta = meta_tokens.astype(F32)
    x_dec = x_sample.reshape(nd, d)
    zeros_d = jnp.zeros((1, d), F32)
    states = state_ssm.reshape(state_ssm.shape[0], nd, hp, SSD_D_STATE)
    dec_states = None

    def row(v):
        return v.reshape(1, -1).astype(F32)

    def pad_lanes(v):
        return jnp.pad(v.reshape(1, -1).astype(F32), ((0, 0), (0, LANES - v.shape[-1])))

    p_ssm, p_ssm_conv, p_sconv, p_cconv = [], [], [], []
    s_ssm_conv, s_sconv, s_cconv = [], [], []

    w_gate_all, w_up_all, w_down_all = ffn_w_gate.astype(BF16), ffn_w_up.astype(BF16), ffn_w_down.astype(BF16)
    ssd_w_in_all = jnp.pad(ssd_w_in, ((0, 0), (0, 0), (0, ssd_n - ssd_w_in.shape[2]))).astype(BF16)
    ssd_w_out_all = ssd_w_out.astype(BF16)

    for i in range(depth):
        kind, j = i % 3, i // 3
        last = i == depth - 1
        g_mix = row(norm_mix[i])
        ffn_w = (row(norm_ffn[i]), _Stacked(w_gate_all, i), _Stacked(w_up_all, i), _Stacked(w_down_all, i),
                 row(norm_final))
        if kind == 0:
            w_in = _Stacked(ssd_w_in_all, j)
            conv_w, conv_b = ssd_conv_w[j], row(ssd_conv_b[j])
            dt_bias, a_log = pad_lanes(ssd_dt_bias[j]), pad_lanes(ssd_a_log[j])
            d_exp = jnp.repeat(ssd_d[j].astype(F32), SSD_HEAD_DIM).reshape(1, d_inner)
            norm_w = row(ssd_norm_w[j])
            w_out, b_out = _Stacked(ssd_w_out_all, j), zeros_d
            mix_w = (g_mix, w_in, conv_w, conv_b, dt_bias, a_log, d_exp, norm_w, w_out)
            zx_d = norm_matmul(x_dec, g_mix, w_in, jnp.zeros((1, ssd_n), F32), nd, _col_tile(ssd_n, 1024))
            cbuf = state_ssm_conv[j].reshape(nd, 3 * conv_dim)
            xs, xdt_t, bm, cm, da, cnew = ssd_dec_pre(zx_d, cbuf, conv_w, conv_b, dt_bias, a_log, d_inner)
            dec_states, y_d = ssd_dec_state(states, j, dec_states, xdt_t, bm, cm, da)
            h_d = ssd_dec_post(y_d, xs, zx_d, d_exp, norm_w)
            s_ssm_conv.append(cnew.reshape(nd, 3, conv_dim))
            x_pad = jnp.pad(x_meta, ((SSD_CHUNK - n_meta, 0), (0, 0)))
            x_pad, st_m, ct_m = ssd_layer(x_pad, mix_w, ffn_w, jnp.zeros((1, hp, SSD_D_STATE), F32),
                                          jnp.zeros((1, SUBLANES, conv_dim), F32), 1, SSD_CHUNK,
                                          SSD_CHUNK - n_meta, False)
            x_meta = x_pad[SSD_CHUNK - n_meta:]
            x_main, st_p, ct_p = ssd_layer(x_main, mix_w, ffn_w, st_m, ct_m, nb, seq, 0, last)
            p_ssm.append(st_p.reshape((nb,) + state_ssm.shape[2:]))
            p_ssm_conv.append(ct_p[:, SUBLANES - 3:, :])
        elif kind == 1:
            w_in = sc_w_in[j].astype(BF16)
            conv_w = sc_conv_w[j]
            w_out, b_out = sc_w_out[j].astype(BF16), zeros_d
            mix_w = (g_mix, w_in, conv_w, w_out)
            bcx_d = norm_matmul(x_dec, g_mix, w_in, jnp.zeros((1, 3 * d), F32), nd, 1024)
            h_d, bnew = sc_dec(bcx_d, conv_w, state_sconv[j].reshape(nd, 2 * d))
            s_sconv.append(bnew.reshape(nd, 2, d))
            x_meta, ct_m = sc_layer(x_meta, mix_w, ffn_w, jnp.zeros((1, SUBLANES, d), F32), 1, n_meta, n_meta, False)
            x_main, ct_p = sc_layer(x_main, mix_w, ffn_w, ct_m, nb, seq, LAYER_ROWS, last)
            p_sconv.append(ct_p[:, SUBLANES - 2:, :])
        else:
            w_in = cf_w_pw1[j].astype(BF16)
            dw_w = cf_dw_w[j]
            taps = dw_w.shape[0]
            w_out, b_out = cf_w_pw2[j].astype(BF16), row(cf_b_pw2[j])
            mix_w = (g_mix, w_in, row(cf_b_pw1[j]), dw_w, row(cf_dw_b[j]), row(cf_ln_g[j]), row(cf_ln_b[j]),
                     w_out, b_out)
            a_d = norm_matmul(x_dec, g_mix, w_in, row(cf_b_pw1[j]), nd, 1024)
            h_d, bnew = cf_dec(a_d, dw_w, row(cf_dw_b[j]), row(cf_ln_g[j]), row(cf_ln_b[j]),
                               state_cconv[j].reshape(nd, (taps - 1) * d))
            s_cconv.append(bnew.reshape(nd, taps - 1, d))
            x_meta, ct_m = cf_layer(x_meta, mix_w, ffn_w, jnp.zeros((1, CF_CARRY_ROWS, d), F32), 1, n_meta, n_meta,
                                    False)
            x_main, ct_p = cf_layer(x_main, mix_w, ffn_w, ct_m, nb, seq, LAYER_ROWS, last)
            p_cconv.append(ct_p[:, CF_CARRY_ROWS - (taps - 1):, :])
        x_dec = ffn(matmul_res(h_d, w_out, b_out, x_dec, nd), ffn_w, nd, last)

    y_prompt = x_main.reshape(nb, seq, d)
    y_sample = x_dec.reshape(nd, 1, d)
    s_ssm = dec_states.reshape(state_ssm.shape)
    return (y_prompt, y_sample, jnp.stack(p_ssm), jnp.stack(p_ssm_conv), jnp.stack(p_sconv), jnp.stack(p_cconv),
            s_ssm, jnp.stack(s_ssm_conv), jnp.stack(s_sconv), jnp.stack(s_cconv))
```

```python
import functools

import jax
import jax.numpy as jnp
from jax.experimental import pallas as pl
from jax.experimental.pallas import tpu as pltpu

F32 = jnp.float32
BF16 = jnp.bfloat16

EPS = 1e-6
LN_EPS = 1e-5
SSD_HEAD_DIM = 64
SSD_D_STATE = 128
SSD_GROUPS = 8
SSD_CHUNK = 256
SSD_SUBCHUNK = 128
HEADS_PER_GROUP = 4
GROUP_WIDTH = HEADS_PER_GROUP * SSD_HEAD_DIM
LANES = 128
SUBLANES = 8
MXU_WIDTH = 256
CF_CARRY_ROWS = 32
DEC_SEQ_BLOCK = 8
LAYER_ROWS = 512
VMEM_LIMIT_BYTES = 56 * 1024 * 1024
SSD_LAYER_VMEM_BYTES = 60 * 1024 * 1024


def _cparams(*sem, vmem=VMEM_LIMIT_BYTES):
    return pltpu.CompilerParams(dimension_semantics=sem, vmem_limit_bytes=vmem)


def _sigmoid(x):
    return 1.0 / (1.0 + jnp.exp(-x))


def _silu(x):
    return x * _sigmoid(x)


def _softplus(x):
    return jnp.maximum(x, 0.0) + jnp.log1p(jnp.exp(-jnp.abs(x)))


def _rms_rows(x, g):
    return x * jax.lax.rsqrt(jnp.mean(x * x, axis=-1, keepdims=True) + EPS) * g


def _col_tile(n, limit):
    return max(t for t in range(LANES, limit + 1, LANES) if n % t == 0)


def _ff_chunks(dff):
    step = 2 * MXU_WIDTH
    return tuple((lo, min(lo + step, dff)) for lo in range(0, dff, step))


class _Stacked:
    def __init__(self, array, layer):
        self.array, self.layer, self.shape = array, layer, array.shape[1:]


def _operand(w):
    return w.array if isinstance(w, _Stacked) else w


def _resident(w):
    if isinstance(w, _Stacked):
        return pl.BlockSpec((None,) + w.shape, lambda *_: (w.layer,) + (0,) * len(w.shape),
                            pipeline_mode=pl.Buffered(1))
    return pl.BlockSpec(w.shape, lambda *_: (0,) * len(w.shape), pipeline_mode=pl.Buffered(1))


def _norm_matmul_kernel(x_ref, g_ref, w_ref, b_ref, o_ref, hn_ref):
    @pl.when(pl.program_id(1) == 0)
    def _():
        hn_ref[...] = _rms_rows(x_ref[...], g_ref[...]).astype(BF16)

    o_ref[...] = jnp.dot(hn_ref[...], w_ref[...], preferred_element_type=F32) + b_ref[...]


def norm_matmul(x, g, w, b, tm, tn):
    m, d = x.shape
    n = w.shape[1]
    if isinstance(w, _Stacked):
        w_spec = pl.BlockSpec((None, d, tn), lambda i, j: (w.layer, 0, j))
    else:
        w_spec = pl.BlockSpec((d, tn), lambda i, j: (0, j))
    return pl.pallas_call(
        _norm_matmul_kernel,
        grid=(m // tm, n // tn),
        in_specs=[
            pl.BlockSpec((tm, d), lambda i, j: (i, 0)),
            pl.BlockSpec((1, d), lambda i, j: (0, 0)),
            w_spec,
            pl.BlockSpec((1, tn), lambda i, j: (0, j)),
        ],
        out_specs=pl.BlockSpec((tm, tn), lambda i, j: (i, j)),
        out_shape=jax.ShapeDtypeStruct((m, n), F32),
        scratch_shapes=[pltpu.VMEM((tm, d), BF16)],
        compiler_params=_cparams("parallel", "arbitrary"),
        name="norm_matmul",
    )(x, g, _operand(w), b)


def _matmul_res_kernel(h_ref, w_ref, b_ref, x_ref, o_ref):
    o_ref[...] = x_ref[...] + b_ref[...] + jnp.dot(h_ref[...], w_ref[...], preferred_element_type=F32)


def matmul_res(h, w, b, x, tm):
    m, k = h.shape
    d = w.shape[1]
    return pl.pallas_call(
        _matmul_res_kernel,
        grid=(m // tm,),
        in_specs=[
            pl.BlockSpec((tm, k), lambda i: (i, 0)),
            _resident(w),
            _resident(b),
            pl.BlockSpec((tm, d), lambda i: (i, 0)),
        ],
        out_specs=pl.BlockSpec((tm, d), lambda i: (i, 0)),
        out_shape=jax.ShapeDtypeStruct((m, d), F32),
        compiler_params=_cparams("parallel"),
        name="matmul_res",
    )(h, _operand(w), b, x)


def _ffn_steps(x_ref, g_ref, wg_ref, wu_ref, wd_ref, gf_ref, o_ref, acc_ref, final_norm):
    held = {}

    def start():
        x = x_ref[...]
        held["hn"] = _rms_rows(x, g_ref[...]).astype(BF16)
        acc_ref[...] = x

    def chunk(lo, hi):
        def gate():
            held["gate"] = jnp.dot(held["hn"], wg_ref[:, lo:hi], preferred_element_type=F32)

        def up():
            up_proj = jnp.dot(held["hn"], wu_ref[:, lo:hi], preferred_element_type=F32)
            held["act"] = (_silu(held["gate"]) * up_proj).astype(BF16)

        def down():
            acc_ref[...] += jnp.dot(held["act"], wd_ref[lo:hi, :], preferred_element_type=F32)
        return [gate, up, down]

    def finish():
        y = acc_ref[...]
        if final_norm:
            y = _rms_rows(y, gf_ref[...])
        o_ref[...] = y

    return [start] + [step for lo, hi in _ff_chunks(wg_ref.shape[1]) for step in chunk(lo, hi)] + [finish]


class _Drip:
    def __init__(self, steps, slots):
        self.steps, self.slots, self.ticks, self.done = steps, slots, 0, 0

    def tick(self):
        self.ticks += 1
        self._run_to(self.ticks * len(self.steps) // self.slots)

    def flush(self):
        self._run_to(len(self.steps))

    def _run_to(self, target):
        while self.done < min(target, len(self.steps)):
            self.steps[self.done]()
            self.done += 1


def _ffn_kernel(x_ref, g_ref, wg_ref, wu_ref, wd_ref, gf_ref, o_ref, acc_ref, *, final_norm):
    for step in _ffn_steps(x_ref, g_ref, wg_ref, wu_ref, wd_ref, gf_ref, o_ref, acc_ref, final_norm):
        step()


def _resident_all(ws):
    return [_resident(w) for w in ws]


def _operands(ws):
    return [_operand(w) for w in ws]


def ffn(x, ffn_w, tm, final_norm):
    m, d = x.shape
    dff = ffn_w[1].shape[1]
    return pl.pallas_call(
        functools.partial(_ffn_kernel, final_norm=final_norm),
        grid=(m // tm,),
        in_specs=[pl.BlockSpec((tm, d), lambda i: (i, 0))] + _resident_all(ffn_w),
        out_specs=pl.BlockSpec((tm, d), lambda i: (i, 0)),
        out_shape=jax.ShapeDtypeStruct((m, d), F32),
        scratch_shapes=[pltpu.VMEM((tm, d), F32)],
        compiler_params=_cparams("parallel"),
        name="ffn",
    )(x, *_operands(ffn_w))


def _lag_specs(tiles, rows, d):
    x_spec = pl.BlockSpec((rows, d), lambda t: (jnp.minimum(t, tiles - 1), 0))
    o_spec = pl.BlockSpec((rows, d), lambda t: (jnp.maximum(t - 1, 0), 0))
    return x_spec, o_spec


def _seq_out_spec(shape, tiles, per_seq):
    return pl.BlockSpec((1,) + shape, lambda t: (jnp.minimum(t, tiles - 1) // per_seq, 0, 0))


def _seq_init_spec(shape, shared, tiles, per_seq):
    if shared:
        return pl.BlockSpec((1,) + shape, lambda t: (0, 0, 0), pipeline_mode=pl.Buffered(1))
    return _seq_out_spec(shape, tiles, per_seq)


def _cumsum_rows(x):
    rows = x.shape[0]
    row = jax.lax.broadcasted_iota(jnp.int32, x.shape, 0)
    shift = 1
    while shift < rows:
        x = x + jnp.where(row >= shift, pltpu.roll(x, shift, axis=0), 0.0)
        shift *= 2
    return x


def _expand_heads4(cols, lane):
    out = cols[:, 3:4]
    for j in (2, 1, 0):
        out = jnp.where(lane < (j + 1) * SSD_HEAD_DIM, cols[:, j:j + 1], out)
    return out


def _ssd_layer_kernel(x_ref, gm_ref, win_ref, cw_ref, cb_ref, dtb_ref, alog_ref, dexp_ref, nw_ref, wout_ref,
                      s0_ref, c0_ref, gf_ref, wg_ref, wu_ref, wd_ref, gl_ref,
                      o_ref, sn_ref, cn_ref,
                      xmid_ref, acc_ref, st_ref, xpad_ref, z_ref, y_ref,
                      *, rows, sub, n_pad, tiles, per_seq, final_norm):
    t = pl.program_id(0)
    c = jax.lax.rem(t, per_seq)
    d_inner = z_ref.shape[1]
    conv_dim = xpad_ref.shape[1]
    n = SSD_D_STATE
    gw = GROUP_WIDTH
    pw = 2 * MXU_WIDTH

    @pl.when(t == 0)
    def _():
        xmid_ref[...] = jnp.zeros_like(xmid_ref)

    @pl.when(c == 0)
    def _():
        for g in range(SSD_GROUPS):
            st_ref[g] = s0_ref[0, g * gw:(g + 1) * gw, :].T
        xpad_ref[0:SUBLANES, :] = c0_ref[0]

    conv_tick = 8
    drip = _Drip(_ffn_steps(xmid_ref, gf_ref, wg_ref, wu_ref, wd_ref, gl_ref, o_ref, acc_ref, final_norm),
                 conv_dim // LANES // conv_tick + (rows // sub) * SSD_GROUPS)

    x = x_ref[...]
    hn = _rms_rows(x, gm_ref[...]).astype(BF16)
    for s in range(d_inner // pw):
        z_ref[:, s * pw:(s + 1) * pw] = jnp.dot(hn, win_ref[:, s * pw:(s + 1) * pw], preferred_element_type=F32)
    for s in range(conv_dim // pw):
        xpad_ref[SUBLANES:SUBLANES + rows, s * pw:(s + 1) * pw] = jnp.dot(
            hn, win_ref[:, d_inner + s * pw:d_inner + (s + 1) * pw], preferred_element_type=F32)
    dt_raw = jnp.dot(hn, win_ref[:, d_inner + conv_dim:d_inner + conv_dim + LANES], preferred_element_type=F32)

    carry = xpad_ref[rows:rows + SUBLANES, :]
    rb = min(rows, 128)
    for s in range(conv_dim // LANES):
        sl = slice(s * LANES, (s + 1) * LANES)
        for r0 in reversed(range(0, rows, rb)):
            base = xpad_ref[r0:r0 + rb + SUBLANES, sl]
            acc = cb_ref[:, sl] + cw_ref[3:4, sl] * base[SUBLANES:SUBLANES + rb, :]
            for k in range(3):
                win = pltpu.roll(base, rb + 3 - k, axis=0)
                acc = acc + cw_ref[k:k + 1, sl] * win[0:rb, :]
            xpad_ref[SUBLANES + r0:SUBLANES + r0 + rb, sl] = _silu(acc)
        if s % conv_tick == conv_tick - 1:
            drip.tick()
    xpad_ref[0:SUBLANES, :] = carry

    dt_all = _softplus(dt_raw + dtb_ref[...])
    if n_pad:
        prow = jax.lax.broadcasted_iota(jnp.int32, dt_all.shape, 0)
        dt_all = jnp.where(prow >= n_pad, dt_all, 0.0)
    a = -jnp.exp(alog_ref[...])

    ti = jax.lax.broadcasted_iota(jnp.int32, (sub, sub), 0)
    si = jax.lax.broadcasted_iota(jnp.int32, (sub, sub), 1)
    causal = ti >= si
    lane_g = jax.lax.broadcasted_iota(jnp.int32, (sub, gw), 1)
    lane_1 = jax.lax.broadcasted_iota(jnp.int32, (1, gw), 1)

    def block_decays(u):
        dt = dt_all[u * sub:(u + 1) * sub, :]
        cs = _cumsum_rows(dt * a)
        cs_last = cs[sub - 1:sub, :]
        return (cs, cs.T, dt.T,
                jnp.exp(cs),
                dt * jnp.exp(cs_last - cs),
                jnp.exp(cs_last))

    def group_bc(u, g):
        rs = slice(SUBLANES + u * sub, SUBLANES + (u + 1) * sub)
        bg_t = xpad_ref[rs, d_inner + g * n:d_inner + (g + 1) * n].T.astype(BF16)
        c_lo = d_inner + SSD_GROUPS * n + g * n
        cg_b = xpad_ref[rs, c_lo:c_lo + n].astype(BF16)
        return bg_t, cg_b, jnp.dot(cg_b, bg_t, preferred_element_type=F32)

    blocks = range(rows // sub)
    decays = [block_decays(u) for u in blocks]
    bcs = [[group_bc(u, g) for g in range(SSD_GROUPS)] for u in blocks]
    for u in blocks:
        rs = slice(SUBLANES + u * sub, SUBLANES + (u + 1) * sub)
        us = slice(u * sub, (u + 1) * sub)
        cs, cs_t, dt_t, e_cs, w_end, e_last = decays[u]
        for g in range(SSD_GROUPS):
            gsl = slice(g * gw, (g + 1) * gw)
            hsl = slice(g * HEADS_PER_GROUP, (g + 1) * HEADS_PER_GROUP)
            xg = xpad_ref[rs, gsl]
            xg_b = xg.astype(BF16)
            bg_t, cg_b, cb = bcs[u][g]
            st = st_ref[g]
            y = jnp.dot(cg_b, st.astype(BF16), preferred_element_type=F32) * _expand_heads4(e_cs[:, hsl], lane_g)
            ms, xms = [], []
            for j in range(HEADS_PER_GROUP):
                h = g * HEADS_PER_GROUP + j
                seg = cs[:, h:h + 1] - cs_t[h:h + 1, :]
                m = cb * jnp.exp(jnp.where(causal, seg, -jnp.inf)) * dt_t[h:h + 1, :]
                in_head = (lane_g >= j * SSD_HEAD_DIM) & (lane_g < (j + 1) * SSD_HEAD_DIM)
                ms.append(m.astype(BF16))
                xms.append(jnp.where(in_head, xg_b, jnp.zeros_like(xg_b)))
            y = y + jnp.dot(jnp.concatenate(ms, axis=1), jnp.concatenate(xms, axis=0), preferred_element_type=F32)
            y = y + dexp_ref[:, gsl] * xg
            wg = (xg * _expand_heads4(w_end[:, hsl], lane_g)).astype(BF16)
            st_ref[g] = st * _expand_heads4(e_last[:, hsl], lane_1) + jnp.dot(bg_t, wg, preferred_element_type=F32)
            y = y * _silu(z_ref[us, gsl])
            y_ref[us, gsl] = _rms_rows(y, nw_ref[:, gsl]).astype(BF16)
            drip.tick()

    drip.flush()
    xmid_ref[...] = x + jnp.dot(y_ref[...], wout_ref[...], preferred_element_type=F32)

    @pl.when((c == per_seq - 1) & (t < tiles))
    def _():
        for g in range(SSD_GROUPS):
            sn_ref[0, g * gw:(g + 1) * gw, :] = st_ref[g].T
        cn_ref[0] = carry


def ssd_layer(x, mix_w, ffn_w, s0, c0, nb, seq, n_pad, final_norm):
    rows = min(SSD_CHUNK, seq)
    assert rows % SSD_SUBCHUNK == 0
    d = x.shape[1]
    w_in, conv_w, norm_w, w_out = mix_w[1], mix_w[2], mix_w[7], mix_w[8]
    d_inner = norm_w.shape[1]
    conv_dim = conv_w.shape[1]
    dff = ffn_w[1].shape[1]
    hp = s0.shape[1]
    per_seq = seq // rows
    tiles = nb * per_seq
    shared0 = s0.shape[0] == 1
    assert w_in.shape[1] == d_inner + conv_dim + LANES
    x_spec, o_spec = _lag_specs(tiles, rows, d)
    return pl.pallas_call(
        functools.partial(_ssd_layer_kernel, rows=rows, sub=SSD_SUBCHUNK, n_pad=n_pad, tiles=tiles, per_seq=per_seq,
                          final_norm=final_norm),
        grid=(tiles + 1,),
        in_specs=[x_spec] + _resident_all(mix_w) + [
            _seq_init_spec((hp, SSD_D_STATE), shared0, tiles, per_seq),
            _seq_init_spec((SUBLANES, conv_dim), shared0, tiles, per_seq),
        ] + _resident_all(ffn_w),
        out_specs=[
            o_spec,
            _seq_out_spec((hp, SSD_D_STATE), tiles, per_seq),
            _seq_out_spec((SUBLANES, conv_dim), tiles, per_seq),
        ],
        out_shape=[
            jax.ShapeDtypeStruct((nb * seq, d), F32),
            jax.ShapeDtypeStruct((nb, hp, SSD_D_STATE), F32),
            jax.ShapeDtypeStruct((nb, SUBLANES, conv_dim), F32),
        ],
        scratch_shapes=[
            pltpu.VMEM((rows, d), F32),
            pltpu.VMEM((rows, d), F32),
            pltpu.VMEM((SSD_GROUPS, SSD_D_STATE, GROUP_WIDTH), F32),
            pltpu.VMEM((rows + SUBLANES, conv_dim), F32),
            pltpu.VMEM((rows, d_inner), F32),
            pltpu.VMEM((rows, d_inner), BF16),
        ],
        compiler_params=_cparams("arbitrary", vmem=SSD_LAYER_VMEM_BYTES),
        name="ssd_layer",
    )(x, *_operands(mix_w), s0, c0, *_operands(ffn_w))


def _sc_layer_kernel(x_ref, gm_ref, win_ref, cw_ref, wout_ref, c0_ref, gf_ref, wg_ref, wu_ref, wd_ref, gl_ref,
                     o_ref, cn_ref, xmid_ref, acc_ref, ppad_ref, *, rows, tiles, per_seq, final_norm):
    t = pl.program_id(0)
    c = jax.lax.rem(t, per_seq)

    @pl.when(t == 0)
    def _():
        xmid_ref[...] = jnp.zeros_like(xmid_ref)

    @pl.when(c == 0)
    def _():
        ppad_ref[0:SUBLANES, :] = c0_ref[0]

    for step in _ffn_steps(xmid_ref, gf_ref, wg_ref, wu_ref, wd_ref, gl_ref, o_ref, acc_ref, final_norm):
        step()

    x = x_ref[...]
    d = x.shape[1]
    hn = _rms_rows(x, gm_ref[...]).astype(BF16)
    p = (jnp.dot(hn, win_ref[:, d:2 * d], preferred_element_type=F32)
         * jnp.dot(hn, win_ref[:, 2 * d:3 * d], preferred_element_type=F32))
    ppad_ref[SUBLANES:SUBLANES + rows, :] = p
    v = (cw_ref[0:1, :] * ppad_ref[SUBLANES - 2:SUBLANES - 2 + rows, :]
         + cw_ref[1:2, :] * ppad_ref[SUBLANES - 1:SUBLANES - 1 + rows, :]
         + cw_ref[2:3, :] * p)
    h = (jnp.dot(hn, win_ref[:, 0:d], preferred_element_type=F32) * v).astype(BF16)
    xmid_ref[...] = x + jnp.dot(h, wout_ref[...], preferred_element_type=F32)
    carry = ppad_ref[rows:rows + SUBLANES, :]
    ppad_ref[0:SUBLANES, :] = carry

    @pl.when((c == per_seq - 1) & (t < tiles))
    def _():
        cn_ref[0] = carry


def sc_layer(x, mix_w, ffn_w, c0, nb, seq, rows, final_norm):
    d = x.shape[1]
    dff = ffn_w[1].shape[1]
    per_seq = seq // rows
    tiles = nb * per_seq
    x_spec, o_spec = _lag_specs(tiles, rows, d)
    return pl.pallas_call(
        functools.partial(_sc_layer_kernel, rows=rows, tiles=tiles, per_seq=per_seq, final_norm=final_norm),
        grid=(tiles + 1,),
        in_specs=[x_spec] + _resident_all(mix_w) + [
            _seq_init_spec((SUBLANES, d), c0.shape[0] == 1, tiles, per_seq),
        ] + _resident_all(ffn_w),
        out_specs=[o_spec, _seq_out_spec((SUBLANES, d), tiles, per_seq)],
        out_shape=[jax.ShapeDtypeStruct((nb * seq, d), F32), jax.ShapeDtypeStruct((nb, SUBLANES, d), F32)],
        scratch_shapes=[pltpu.VMEM((rows, d), F32), pltpu.VMEM((rows, d), F32), pltpu.VMEM((rows + SUBLANES, d), F32)],
        compiler_params=_cparams("arbitrary"),
        name="sc_layer",
    )(x, *_operands(mix_w), c0, *_operands(ffn_w))


def _cf_layer_kernel(x_ref, gm_ref, w1_ref, b1_ref, dw_ref, db_ref, lg_ref, lb_ref, w2_ref, b2_ref, c0_ref,
                     gf_ref, wg_ref, wu_ref, wd_ref, gl_ref,
                     o_ref, cn_ref, xmid_ref, acc_ref, upad_ref, v_ref, *, rows, tiles, per_seq, final_norm):
    t = pl.program_id(0)
    c = jax.lax.rem(t, per_seq)
    taps = dw_ref.shape[0]
    hist = CF_CARRY_ROWS
    first = hist - (taps - 1)

    @pl.when(t == 0)
    def _():
        xmid_ref[...] = jnp.zeros_like(xmid_ref)

    @pl.when(c == 0)
    def _():
        upad_ref[0:hist, :] = c0_ref[0]

    x = x_ref[...]
    d = x.shape[1]
    rb = min(rows, 128)
    drip = _Drip(_ffn_steps(xmid_ref, gf_ref, wg_ref, wu_ref, wd_ref, gl_ref, o_ref, acc_ref, final_norm),
                 (d // LANES) * (rows // rb))
    hn = _rms_rows(x, gm_ref[...]).astype(BF16)
    u = ((jnp.dot(hn, w1_ref[:, 0:d], preferred_element_type=F32) + b1_ref[:, 0:d])
         * _sigmoid(jnp.dot(hn, w1_ref[:, d:2 * d], preferred_element_type=F32) + b1_ref[:, d:2 * d]))
    upad_ref[hist:hist + rows, :] = u
    for s in range(d // LANES):
        sl = slice(s * LANES, (s + 1) * LANES)
        for r0 in range(0, rows, rb):
            base = upad_ref[r0:r0 + rb + hist, sl]
            acc = jnp.broadcast_to(db_ref[:, sl], (rb, LANES))
            for phase in range(SUBLANES):
                steps = [q for q in range(hist // SUBLANES + 1) if first <= q * SUBLANES + phase <= hist]
                win = base if phase == 0 else pltpu.roll(base, rb + hist - phase, axis=0)
                for q in steps:
                    k = q * SUBLANES + phase - first
                    acc = acc + dw_ref[k:k + 1, sl] * win[q * SUBLANES:q * SUBLANES + rb, :]
            v_ref[r0:r0 + rb, sl] = acc
            drip.tick()
    drip.flush()
    v = v_ref[...]
    mu = jnp.mean(v, axis=-1, keepdims=True)
    dv = v - mu
    var = jnp.mean(dv * dv, axis=-1, keepdims=True)
    y = dv * jax.lax.rsqrt(var + LN_EPS) * lg_ref[...] + lb_ref[...]
    h = _silu(y).astype(BF16)
    xmid_ref[...] = x + b2_ref[...] + jnp.dot(h, w2_ref[...], preferred_element_type=F32)
    carry = upad_ref[rows:rows + hist, :]
    upad_ref[0:hist, :] = carry

    @pl.when((c == per_seq - 1) & (t < tiles))
    def _():
        cn_ref[0] = carry


def cf_layer(x, mix_w, ffn_w, c0, nb, seq, rows, final_norm):
    d = x.shape[1]
    dff = ffn_w[1].shape[1]
    per_seq = seq // rows
    tiles = nb * per_seq
    assert mix_w[3].shape[0] - 1 <= CF_CARRY_ROWS
    x_spec, o_spec = _lag_specs(tiles, rows, d)
    return pl.pallas_call(
        functools.partial(_cf_layer_kernel, rows=rows, tiles=tiles, per_seq=per_seq, final_norm=final_norm),
        grid=(tiles + 1,),
        in_specs=[x_spec] + _resident_all(mix_w) + [
            _seq_init_spec((CF_CARRY_ROWS, d), c0.shape[0] == 1, tiles, per_seq),
        ] + _resident_all(ffn_w),
        out_specs=[o_spec, _seq_out_spec((CF_CARRY_ROWS, d), tiles, per_seq)],
        out_shape=[jax.ShapeDtypeStruct((nb * seq, d), F32), jax.ShapeDtypeStruct((nb, CF_CARRY_ROWS, d), F32)],
        scratch_shapes=[pltpu.VMEM((rows, d), F32), pltpu.VMEM((rows, d), F32),
                        pltpu.VMEM((rows + CF_CARRY_ROWS, d), F32), pltpu.VMEM((rows, d), F32)],
        compiler_params=_cparams("arbitrary"),
        name="cf_layer",
    )(x, *_operands(mix_w), c0, *_operands(ffn_w))


def _expand_heads_mxu(v, d_inner):
    hrow = jax.lax.broadcasted_iota(jnp.int32, (LANES, d_inner), 0)
    col = jax.lax.broadcasted_iota(jnp.int32, (LANES, d_inner), 1)
    onehot = jnp.where((col >= hrow * SSD_HEAD_DIM) & (col < (hrow + 1) * SSD_HEAD_DIM), 1.0, 0.0).astype(BF16)
    hi = v.astype(BF16)
    r1 = v - hi.astype(F32)
    mid = r1.astype(BF16)
    lo = (r1 - mid.astype(F32)).astype(BF16)
    out = jnp.dot(hi, onehot, preferred_element_type=F32)
    out = out + jnp.dot(mid, onehot, preferred_element_type=F32)
    return out + jnp.dot(lo, onehot, preferred_element_type=F32)


def _ssd_dec_pre_kernel(xa_ref, xb_ref, dt_ref, cbuf_ref, cw_ref, cb_ref, dtb_ref, alog_ref,
                        xs_ref, xdt_t_ref, b_ref, c_ref, da_ref, cnew_ref):
    conv_dim = cw_ref.shape[1]
    d_inner = xs_ref.shape[1]
    gn = b_ref.shape[1]

    def conv_silu(xnew, lo, hi):
        acc = cb_ref[:, lo:hi] + cw_ref[3:4, lo:hi] * xnew
        for k in range(3):
            acc = acc + cw_ref[k:k + 1, lo:hi] * cbuf_ref[:, k * conv_dim + lo:k * conv_dim + hi]
        return _silu(acc)

    xa = xa_ref[...]
    xb = xb_ref[...]
    cnew_ref[:, 0:2 * conv_dim] = cbuf_ref[:, conv_dim:3 * conv_dim]
    cnew_ref[:, 2 * conv_dim:2 * conv_dim + d_inner] = xa
    cnew_ref[:, 2 * conv_dim + d_inner:3 * conv_dim] = xb
    dt = _softplus(dt_ref[...] + dtb_ref[...])
    da_ref[...] = jnp.exp(dt * -jnp.exp(alog_ref[...]))
    xs = conv_silu(xa, 0, d_inner)
    xs_ref[...] = xs
    xdt_t_ref[...] = (xs * _expand_heads_mxu(dt, d_inner)).T.astype(BF16)
    bc = conv_silu(xb, d_inner, conv_dim)
    b_ref[...] = bc[:, 0:gn]
    c_ref[...] = bc[:, gn:2 * gn]


def ssd_dec_pre(zx, cbuf, conv_w, conv_b, dt_bias, a_log, d_inner):
    nb = zx.shape[0]
    conv_dim = conv_w.shape[1]
    assert conv_dim == 2 * d_inner
    gn = (conv_dim - d_inner) // 2
    dt_blk = (d_inner + conv_dim) // LANES

    def full(shape):
        return pl.BlockSpec(shape, lambda i: (0, 0))

    return pl.pallas_call(
        _ssd_dec_pre_kernel,
        grid=(1,),
        in_specs=[
            pl.BlockSpec((nb, d_inner), lambda i: (0, 1)),
            pl.BlockSpec((nb, d_inner), lambda i: (0, 2)),
            pl.BlockSpec((nb, LANES), lambda i: (0, dt_blk)),
            full((nb, 3 * conv_dim)), full((4, conv_dim)), full((1, conv_dim)), full((1, LANES)), full((1, LANES)),
        ],
        out_specs=[
            full((nb, d_inner)), full((d_inner, nb)), full((nb, gn)), full((nb, gn)), full((nb, LANES)),
            full((nb, 3 * conv_dim)),
        ],
        out_shape=[
            jax.ShapeDtypeStruct((nb, d_inner), F32), jax.ShapeDtypeStruct((d_inner, nb), BF16),
            jax.ShapeDtypeStruct((nb, gn), F32), jax.ShapeDtypeStruct((nb, gn), F32),
            jax.ShapeDtypeStruct((nb, LANES), F32), jax.ShapeDtypeStruct((nb, 3 * conv_dim), F32),
        ],
        compiler_params=_cparams("arbitrary"),
        name="ssd_dec_pre",
    )(zx, zx, zx, cbuf, conv_w, conv_b, dt_bias, a_log)


def _ssd_dec_state_kernel(*refs, chained):
    if chained:
        refs = refs[1:]
    s_ref, xdt_t_ref, b_ref, c_ref, da_ref, sn_ref, y_ref = refs
    i = pl.program_id(0)
    bs = s_ref.shape[0]
    hp, nb = xdt_t_ref.shape
    n = SSD_D_STATE
    gw = GROUP_WIDTH
    groups = hp // gw
    c_blk = c_ref[...].astype(BF16)
    seq_row = jax.lax.broadcasted_iota(jnp.int32, (nb, n), 0)
    out_row = jax.lax.broadcasted_iota(jnp.int32, (bs, gw), 0)

    def one_seq(k, ys):
        da_row = da_ref[pl.ds(k, 1), :]
        new_ys = []
        for g in range(groups):
            b_sel = jnp.where(seq_row == i * bs + k, b_ref[:, g * n:(g + 1) * n], 0.0).astype(BF16)
            outer = jnp.dot(xdt_t_ref[g * gw:(g + 1) * gw, :], b_sel, preferred_element_type=F32)
            parts = []
            for j in range(HEADS_PER_GROUP):
                h = g * HEADS_PER_GROUP + j
                hsl = slice(g * gw + j * SSD_HEAD_DIM, g * gw + (j + 1) * SSD_HEAD_DIM)
                snew = s_ref[k, hsl, :] * da_row[:, h:h + 1] + outer[j * SSD_HEAD_DIM:(j + 1) * SSD_HEAD_DIM, :]
                sn_ref[k, hsl, :] = snew
                parts.append(snew.astype(BF16))
            y8 = jax.lax.dot_general(c_blk[:, g * n:(g + 1) * n], jnp.concatenate(parts, axis=0),
                                     (((1,), (1,)), ((), ())), preferred_element_type=F32)
            new_ys.append(jnp.where(out_row == k, y8, ys[g]))
        return tuple(new_ys)

    ys = jax.lax.fori_loop(0, bs, one_seq, tuple(jnp.zeros((bs, gw), F32) for _ in range(groups)))
    for g in range(groups):
        y_ref[:, g * gw:(g + 1) * gw] = ys[g]


def ssd_dec_state(states, layer, prev, xdt_t, bm, cm, da):
    n_layers, nb, hp, n = states.shape
    bs = DEC_SEQ_BLOCK
    chained = prev is not None

    def full(shape):
        return pl.BlockSpec(shape, lambda i: (0, 0))

    state_blk = pl.BlockSpec((None, bs, hp, n), lambda i: (layer, i, 0, 0))
    in_specs = [state_blk, full((hp, nb)), full(bm.shape),
                pl.BlockSpec((bs, cm.shape[1]), lambda i: (i, 0)), pl.BlockSpec((bs, LANES), lambda i: (i, 0))]
    args = [states, xdt_t, bm, cm, da]
    if chained:
        in_specs = [pl.BlockSpec(memory_space=pl.ANY)] + in_specs
        args = [prev] + args
    return pl.pallas_call(
        functools.partial(_ssd_dec_state_kernel, chained=chained),
        grid=(nb // bs,),
        in_specs=in_specs,
        out_specs=[state_blk, pl.BlockSpec((bs, hp), lambda i: (i, 0))],
        out_shape=[jax.ShapeDtypeStruct(states.shape, F32), jax.ShapeDtypeStruct((nb, hp), F32)],
        input_output_aliases={0: 0} if chained else {},
        compiler_params=_cparams("arbitrary"),
        name="ssd_dec_state",
    )(*args)


def _ssd_dec_post_kernel(y_ref, xs_ref, z_ref, dexp_ref, nw_ref, h_ref):
    gw = GROUP_WIDTH
    y = (y_ref[...] + dexp_ref[...] * xs_ref[...]) * _silu(z_ref[...])
    for g in range(y.shape[1] // gw):
        gsl = slice(g * gw, (g + 1) * gw)
        h_ref[:, gsl] = _rms_rows(y[:, gsl], nw_ref[:, gsl]).astype(BF16)


def ssd_dec_post(y, xs, zx, d_exp, norm_w):
    nb, d_inner = y.shape

    def full(shape):
        return pl.BlockSpec(shape, lambda i: (0, 0))

    return pl.pallas_call(
        _ssd_dec_post_kernel,
        grid=(1,),
        in_specs=[full((nb, d_inner)), full((nb, d_inner)), full((nb, d_inner)), full((1, d_inner)), full((1, d_inner))],
        out_specs=full((nb, d_inner)),
        out_shape=jax.ShapeDtypeStruct((nb, d_inner), BF16),
        compiler_params=_cparams("arbitrary"),
        name="ssd_dec_post",
    )(y, xs, zx, d_exp, norm_w)


def _sc_dec_kernel(bg_ref, cg_ref, xv_ref, w_ref, buf_ref, h_ref, bn_ref):
    d = w_ref.shape[1]
    p = cg_ref[...] * xv_ref[...]
    v = w_ref[0:1, :] * buf_ref[:, 0:d] + w_ref[1:2, :] * buf_ref[:, d:2 * d] + w_ref[2:3, :] * p
    h_ref[...] = (bg_ref[...] * v).astype(BF16)
    bn_ref[:, 0:d] = buf_ref[:, d:2 * d]
    bn_ref[:, d:2 * d] = p


def sc_dec(bcx, conv_w, buf):
    nb = bcx.shape[0]
    d = conv_w.shape[1]

    def full(shape, col=0):
        return pl.BlockSpec(shape, lambda i: (0, col))

    return pl.pallas_call(
        _sc_dec_kernel,
        grid=(1,),
        in_specs=[full((nb, d), 0), full((nb, d), 1), full((nb, d), 2), full((3, d)), full((nb, 2 * d))],
        out_specs=[full((nb, d)), full((nb, 2 * d))],
        out_shape=[jax.ShapeDtypeStruct((nb, d), BF16), jax.ShapeDtypeStruct((nb, 2 * d), F32)],
        compiler_params=_cparams("arbitrary"),
        name="sc_dec",
    )(bcx, bcx, bcx, conv_w, buf)


def _cf_dec_kernel(a1_ref, a2_ref, w_ref, b_ref, lg_ref, lb_ref, buf_ref, h_ref, bn_ref):
    taps, d = w_ref.shape
    u = a1_ref[...] * _sigmoid(a2_ref[...])
    v = b_ref[...] + w_ref[taps - 1:taps, :] * u
    for k in range(taps - 1):
        v = v + w_ref[k:k + 1, :] * buf_ref[:, k * d:(k + 1) * d]
    mu = jnp.mean(v, axis=-1, keepdims=True)
    dv = v - mu
    var = jnp.mean(dv * dv, axis=-1, keepdims=True)
    y = dv * jax.lax.rsqrt(var + LN_EPS) * lg_ref[...] + lb_ref[...]
    h_ref[...] = _silu(y).astype(BF16)
    bn_ref[:, 0:(taps - 2) * d] = buf_ref[:, d:(taps - 1) * d]
    bn_ref[:, (taps - 2) * d:(taps - 1) * d] = u


def cf_dec(a, dw_w, dw_b, ln_g, ln_b, buf):
    nb = a.shape[0]
    taps, d = dw_w.shape

    def full(shape, col=0):
        return pl.BlockSpec(shape, lambda i: (0, col))

    return pl.pallas_call(
        _cf_dec_kernel,
        grid=(1,),
        in_specs=[full((nb, d), 0), full((nb, d), 1), full((taps, d)), full((1, d)), full((1, d)), full((1, d)),
                  full((nb, (taps - 1) * d))],
        out_specs=[full((nb, d)), full((nb, (taps - 1) * d))],
        out_shape=[jax.ShapeDtypeStruct((nb, d), BF16), jax.ShapeDtypeStruct((nb, (taps - 1) * d), F32)],
        compiler_params=_cparams("arbitrary"),
        name="cf_dec",
    )(a, a, dw_w, dw_b, ln_g, ln_b, buf)


def kernel(x_prompt, x_sample, state_ssm, state_ssm_conv, state_sconv, state_cconv, meta_tokens,
           norm_mix, norm_ffn, norm_final, ssd_w_in, ssd_conv_w, ssd_conv_b, ssd_dt_bias, ssd_a_log,
           ssd_d, ssd_norm_w, ssd_w_out, sc_w_in, sc_conv_w, sc_w_out, cf_w_pw1, cf_b_pw1, cf_dw_w,
           cf_dw_b, cf_ln_g, cf_ln_b, cf_w_pw2, cf_b_pw2, ffn_w_gate, ffn_w_up, ffn_w_down):
    nb, seq, d = x_prompt.shape
    nd = x_sample.shape[0]
    assert x_sample.shape[1] == 1
    depth = norm_mix.shape[0]
    n_meta = meta_tokens.shape[0]
    heads = ssd_a_log.shape[1]
    d_inner = ssd_norm_w.shape[1]
    conv_dim = ssd_conv_w.shape[2]
    hp = heads * SSD_HEAD_DIM
    assert hp == d_inner and heads <= LANES and n_meta <= SSD_SUBCHUNK
    assert seq % SSD_CHUNK == 0 and seq % LAYER_ROWS == 0
    ssd_n = d_inner + conv_dim + LANES

    x_main = x_prompt.reshape(nb * seq, d)
    x_meta = meta_tokens.astype(F32)
    x_dec = x_sample.reshape(nd, d)
    zeros_d = jnp.zeros((1, d), F32)
    states = state_ssm.reshape(state_ssm.shape[0], nd, hp, SSD_D_STATE)
    dec_states = None

    def row(v):
        return v.reshape(1, -1).astype(F32)

    def pad_lanes(v):
        return jnp.pad(v.reshape(1, -1).astype(F32), ((0, 0), (0, LANES - v.shape[-1])))

    p_ssm, p_ssm_conv, p_sconv, p_cconv = [], [], [], []
    s_ssm_conv, s_sconv, s_cconv = [], [], []

    w_gate_all, w_up_all, w_down_all = ffn_w_gate.astype(BF16), ffn_w_up.astype(BF16), ffn_w_down.astype(BF16)
    ssd_w_in_all = jnp.pad(ssd_w_in, ((0, 0), (0, 0), (0, ssd_n - ssd_w_in.shape[2]))).astype(BF16)
    ssd_w_out_all = ssd_w_out.astype(BF16)

    for i in range(depth):
        kind, j = i % 3, i // 3
        last = i == depth - 1
        g_mix = row(norm_mix[i])
        ffn_w = (row(norm_ffn[i]), _Stacked(w_gate_all, i), _Stacked(w_up_all, i), _Stacked(w_down_all, i),
                 row(norm_final))
        if kind == 0:
            w_in = _Stacked(ssd_w_in_all, j)
            conv_w, conv_b = ssd_conv_w[j], row(ssd_conv_b[j])
            dt_bias, a_log = pad_lanes(ssd_dt_bias[j]), pad_lanes(ssd_a_log[j])
            d_exp = jnp.repeat(ssd_d[j].astype(F32), SSD_HEAD_DIM).reshape(1, d_inner)
            norm_w = row(ssd_norm_w[j])
            w_out, b_out = _Stacked(ssd_w_out_all, j), zeros_d
            mix_w = (g_mix, w_in, conv_w, conv_b, dt_bias, a_log, d_exp, norm_w, w_out)
            zx_d = norm_matmul(x_dec, g_mix, w_in, jnp.zeros((1, ssd_n), F32), nd, _col_tile(ssd_n, 1024))
            cbuf = state_ssm_conv[j].reshape(nd, 3 * conv_dim)
            xs, xdt_t, bm, cm, da, cnew = ssd_dec_pre(zx_d, cbuf, conv_w, conv_b, dt_bias, a_log, d_inner)
            dec_states, y_d = ssd_dec_state(states, j, dec_states, xdt_t, bm, cm, da)
            h_d = ssd_dec_post(y_d, xs, zx_d, d_exp, norm_w)
            s_ssm_conv.append(cnew.reshape(nd, 3, conv_dim))
            x_pad = jnp.pad(x_meta, ((SSD_SUBCHUNK - n_meta, 0), (0, 0)))
            x_pad, st_m, ct_m = ssd_layer(x_pad, mix_w, ffn_w, jnp.zeros((1, hp, SSD_D_STATE), F32),
                                          jnp.zeros((1, SUBLANES, conv_dim), F32), 1, SSD_SUBCHUNK,
                                          SSD_SUBCHUNK - n_meta, False)
            x_meta = x_pad[SSD_SUBCHUNK - n_meta:]
            x_main, st_p, ct_p = ssd_layer(x_main, mix_w, ffn_w, st_m, ct_m, nb, seq, 0, last)
            p_ssm.append(st_p.reshape((nb,) + state_ssm.shape[2:]))
            p_ssm_conv.append(ct_p[:, SUBLANES - 3:, :])
        elif kind == 1:
            w_in = sc_w_in[j].astype(BF16)
            conv_w = sc_conv_w[j]
            w_out, b_out = sc_w_out[j].astype(BF16), zeros_d
            mix_w = (g_mix, w_in, conv_w, w_out)
            bcx_d = norm_matmul(x_dec, g_mix, w_in, jnp.zeros((1, 3 * d), F32), nd, 1024)
            h_d, bnew = sc_dec(bcx_d, conv_w, state_sconv[j].reshape(nd, 2 * d))
            s_sconv.append(bnew.reshape(nd, 2, d))
            x_meta, ct_m = sc_layer(x_meta, mix_w, ffn_w, jnp.zeros((1, SUBLANES, d), F32), 1, n_meta, n_meta, False)
            x_main, ct_p = sc_layer(x_main, mix_w, ffn_w, ct_m, nb, seq, LAYER_ROWS, last)
            p_sconv.append(ct_p[:, SUBLANES - 2:, :])
        else:
            w_in = cf_w_pw1[j].astype(BF16)
            dw_w = cf_dw_w[j]
            taps = dw_w.shape[0]
            w_out, b_out = cf_w_pw2[j].astype(BF16), row(cf_b_pw2[j])
            mix_w = (g_mix, w_in, row(cf_b_pw1[j]), dw_w, row(cf_dw_b[j]), row(cf_ln_g[j]), row(cf_ln_b[j]),
                     w_out, b_out)
            a_d = norm_matmul(x_dec, g_mix, w_in, row(cf_b_pw1[j]), nd, 1024)
            h_d, bnew = cf_dec(a_d, dw_w, row(cf_dw_b[j]), row(cf_ln_g[j]), row(cf_ln_b[j]),
                               state_cconv[j].reshape(nd, (taps - 1) * d))
            s_cconv.append(bnew.reshape(nd, taps - 1, d))
            x_meta, ct_m = cf_layer(x_meta, mix_w, ffn_w, jnp.zeros((1, CF_CARRY_ROWS, d), F32), 1, n_meta, n_meta,
                                    False)
            x_main, ct_p = cf_layer(x_main, mix_w, ffn_w, ct_m, nb, seq, LAYER_ROWS, last)
            p_cconv.append(ct_p[:, CF_CARRY_ROWS - (taps - 1):, :])
        x_dec = ffn(matmul_res(h_d, w_out, b_out, x_dec, nd), ffn_w, nd, last)

    y_prompt = x_main.reshape(nb, seq, d)
    y_sample = x_dec.reshape(nd, 1, d)
    s_ssm = dec_states.reshape(state_ssm.shape)
    return (y_prompt, y_sample, jnp.stack(p_ssm), jnp.stack(p_ssm_conv), jnp.stack(p_sconv), jnp.stack(p_cconv),
            s_ssm, jnp.stack(s_ssm_conv), jnp.stack(s_sconv), jnp.stack(s_cconv))
```

```python
import functools

import jax
import jax.numpy as jnp
from jax.experimental import pallas as pl
from jax.experimental.pallas import tpu as pltpu

F32 = jnp.float32
BF16 = jnp.bfloat16

EPS = 1e-6
LN_EPS = 1e-5
SSD_HEAD_DIM = 64
SSD_D_STATE = 128
SSD_GROUPS = 8
SSD_CHUNK = 256
SSD_SUBCHUNK = 128
HEADS_PER_GROUP = 4
GROUP_WIDTH = HEADS_PER_GROUP * SSD_HEAD_DIM
LANES = 128
SUBLANES = 8
MXU_WIDTH = 256
CF_CARRY_ROWS = 32
DEC_SEQ_BLOCK = 8
LAYER_ROWS = 512
VMEM_LIMIT_BYTES = 56 * 1024 * 1024
SSD_LAYER_VMEM_BYTES = 60 * 1024 * 1024


def _cparams(*sem, vmem=VMEM_LIMIT_BYTES):
    return pltpu.CompilerParams(dimension_semantics=sem, vmem_limit_bytes=vmem)


def _sigmoid(x):
    return 1.0 / (1.0 + jnp.exp(-x))


def _silu(x):
    return x * _sigmoid(x)


def _softplus(x):
    return jnp.maximum(x, 0.0) + jnp.log1p(jnp.exp(-jnp.abs(x)))


def _rms_rows(x, g):
    return x * jax.lax.rsqrt(jnp.mean(x * x, axis=-1, keepdims=True) + EPS) * g


def _col_tile(n, limit):
    return max(t for t in range(LANES, limit + 1, LANES) if n % t == 0)


def _ff_chunks(dff):
    step = 2 * MXU_WIDTH
    return tuple((lo, min(lo + step, dff)) for lo in range(0, dff, step))


class _Stacked:
    def __init__(self, array, layer):
        self.array, self.layer, self.shape = array, layer, array.shape[1:]


def _operand(w):
    return w.array if isinstance(w, _Stacked) else w


def _resident(w):
    if isinstance(w, _Stacked):
        return pl.BlockSpec((None,) + w.shape, lambda *_: (w.layer,) + (0,) * len(w.shape),
                            pipeline_mode=pl.Buffered(1))
    return pl.BlockSpec(w.shape, lambda *_: (0,) * len(w.shape), pipeline_mode=pl.Buffered(1))


def _norm_matmul_kernel(x_ref, g_ref, w_ref, b_ref, o_ref, hn_ref):
    @pl.when(pl.program_id(1) == 0)
    def _():
        hn_ref[...] = _rms_rows(x_ref[...], g_ref[...]).astype(BF16)

    o_ref[...] = jnp.dot(hn_ref[...], w_ref[...], preferred_element_type=F32) + b_ref[...]


def norm_matmul(x, g, w, b, tm, tn):
    m, d = x.shape
    n = w.shape[1]
    if isinstance(w, _Stacked):
        w_spec = pl.BlockSpec((None, d, tn), lambda i, j: (w.layer, 0, j))
    else:
        w_spec = pl.BlockSpec((d, tn), lambda i, j: (0, j))
    return pl.pallas_call(
        _norm_matmul_kernel,
        grid=(m // tm, n // tn),
        in_specs=[
            pl.BlockSpec((tm, d), lambda i, j: (i, 0)),
            pl.BlockSpec((1, d), lambda i, j: (0, 0)),
            w_spec,
            pl.BlockSpec((1, tn), lambda i, j: (0, j)),
        ],
        out_specs=pl.BlockSpec((tm, tn), lambda i, j: (i, j)),
        out_shape=jax.ShapeDtypeStruct((m, n), F32),
        scratch_shapes=[pltpu.VMEM((tm, d), BF16)],
        compiler_params=_cparams("parallel", "arbitrary"),
        name="norm_matmul",
    )(x, g, _operand(w), b)


def _matmul_res_kernel(h_ref, w_ref, b_ref, x_ref, o_ref):
    o_ref[...] = x_ref[...] + b_ref[...] + jnp.dot(h_ref[...], w_ref[...], preferred_element_type=F32)


def matmul_res(h, w, b, x, tm):
    m, k = h.shape
    d = w.shape[1]
    return pl.pallas_call(
        _matmul_res_kernel,
        grid=(m // tm,),
        in_specs=[
            pl.BlockSpec((tm, k), lambda i: (i, 0)),
            _resident(w),
            _resident(b),
            pl.BlockSpec((tm, d), lambda i: (i, 0)),
        ],
        out_specs=pl.BlockSpec((tm, d), lambda i: (i, 0)),
        out_shape=jax.ShapeDtypeStruct((m, d), F32),
        compiler_params=_cparams("parallel"),
        name="matmul_res",
    )(h, _operand(w), b, x)


def _ffn_steps(x_ref, g_ref, wg_ref, wu_ref, wd_ref, gf_ref, o_ref, acc_ref, final_norm):
    held = {}

    def start():
        x = x_ref[...]
        held["hn"] = _rms_rows(x, g_ref[...]).astype(BF16)
        acc_ref[...] = x

    def chunk(lo, hi):
        def gate():
            held["gate"] = jnp.dot(held["hn"], wg_ref[:, lo:hi], preferred_element_type=F32)

        def up():
            up_proj = jnp.dot(held["hn"], wu_ref[:, lo:hi], preferred_element_type=F32)
            held["act"] = (_silu(held["gate"]) * up_proj).astype(BF16)

        def down():
            acc_ref[...] += jnp.dot(held["act"], wd_ref[lo:hi, :], preferred_element_type=F32)
        return [gate, up, down]

    def finish():
        y = acc_ref[...]
        if final_norm:
            y = _rms_rows(y, gf_ref[...])
        o_ref[...] = y

    return [start] + [step for lo, hi in _ff_chunks(wg_ref.shape[1]) for step in chunk(lo, hi)] + [finish]


class _Drip:
    def __init__(self, steps, slots):
        self.steps, self.slots, self.ticks, self.done = steps, slots, 0, 0

    def tick(self):
        self.ticks += 1
        self._run_to(self.ticks * len(self.steps) // self.slots)

    def flush(self):
        self._run_to(len(self.steps))

    def _run_to(self, target):
        while self.done < min(target, len(self.steps)):
            self.steps[self.done]()
            self.done += 1


def _ffn_kernel(x_ref, g_ref, wg_ref, wu_ref, wd_ref, gf_ref, o_ref, acc_ref, *, final_norm):
    for step in _ffn_steps(x_ref, g_ref, wg_ref, wu_ref, wd_ref, gf_ref, o_ref, acc_ref, final_norm):
        step()


def _resident_all(ws):
    return [_resident(w) for w in ws]


def _operands(ws):
    return [_operand(w) for w in ws]


def ffn(x, ffn_w, tm, final_norm):
    m, d = x.shape
    dff = ffn_w[1].shape[1]
    return pl.pallas_call(
        functools.partial(_ffn_kernel, final_norm=final_norm),
        grid=(m // tm,),
        in_specs=[pl.BlockSpec((tm, d), lambda i: (i, 0))] + _resident_all(ffn_w),
        out_specs=pl.BlockSpec((tm, d), lambda i: (i, 0)),
        out_shape=jax.ShapeDtypeStruct((m, d), F32),
        scratch_shapes=[pltpu.VMEM((tm, d), F32)],
        compiler_params=_cparams("parallel"),
        name="ffn",
    )(x, *_operands(ffn_w))


def _lag_specs(tiles, rows, d):
    x_spec = pl.BlockSpec((rows, d), lambda t: (jnp.minimum(t, tiles - 1), 0))
    o_spec = pl.BlockSpec((rows, d), lambda t: (jnp.maximum(t - 1, 0), 0))
    return x_spec, o_spec


def _seq_out_spec(shape, tiles, per_seq):
    return pl.BlockSpec((1,) + shape, lambda t: (jnp.minimum(t, tiles - 1) // per_seq, 0, 0))


def _seq_init_spec(shape, shared, tiles, per_seq):
    if shared:
        return pl.BlockSpec((1,) + shape, lambda t: (0, 0, 0), pipeline_mode=pl.Buffered(1))
    return _seq_out_spec(shape, tiles, per_seq)


def _cumsum_rows(x):
    rows = x.shape[0]
    row = jax.lax.broadcasted_iota(jnp.int32, x.shape, 0)
    shift = 1
    while shift < rows:
        x = x + jnp.where(row >= shift, pltpu.roll(x, shift, axis=0), 0.0)
        shift *= 2
    return x


def _expand_heads4(cols, lane):
    out = cols[:, 3:4]
    for j in (2, 1, 0):
        out = jnp.where(lane < (j + 1) * SSD_HEAD_DIM, cols[:, j:j + 1], out)
    return out


def _ssd_layer_kernel(x_ref, gm_ref, win_ref, cw_ref, cb_ref, dtb_ref, alog_ref, dexp_ref, nw_ref, wout_ref,
                      s0_ref, c0_ref, gf_ref, wg_ref, wu_ref, wd_ref, gl_ref,
                      o_ref, sn_ref, cn_ref,
                      xmid_ref, acc_ref, st_ref, xpad_ref, z_ref, y_ref,
                      *, rows, sub, n_pad, tiles, per_seq, final_norm):
    t = pl.program_id(0)
    c = jax.lax.rem(t, per_seq)
    d_inner = z_ref.shape[1]
    conv_dim = xpad_ref.shape[1]
    n = SSD_D_STATE
    gw = GROUP_WIDTH
    pw = 2 * MXU_WIDTH

    @pl.when(t == 0)
    def _():
        xmid_ref[...] = jnp.zeros_like(xmid_ref)

    @pl.when(c == 0)
    def _():
        for g in range(SSD_GROUPS):
            st_ref[g] = s0_ref[0, g * gw:(g + 1) * gw, :].T
        xpad_ref[0:SUBLANES, :] = c0_ref[0]

    conv_tick = 4
    drip = _Drip(_ffn_steps(xmid_ref, gf_ref, wg_ref, wu_ref, wd_ref, gl_ref, o_ref, acc_ref, final_norm),
                 conv_dim // LANES // conv_tick + (rows // sub) * SSD_GROUPS)

    x = x_ref[...]
    hn = _rms_rows(x, gm_ref[...]).astype(BF16)
    for s in range(d_inner // pw):
        z_ref[:, s * pw:(s + 1) * pw] = jnp.dot(hn, win_ref[:, s * pw:(s + 1) * pw], preferred_element_type=F32)
    for s in range(conv_dim // pw):
        xpad_ref[SUBLANES:SUBLANES + rows, s * pw:(s + 1) * pw] = jnp.dot(
            hn, win_ref[:, d_inner + s * pw:d_inner + (s + 1) * pw], preferred_element_type=F32)
    dt_raw = jnp.dot(hn, win_ref[:, d_inner + conv_dim:d_inner + conv_dim + LANES], preferred_element_type=F32)

    carry = xpad_ref[rows:rows + SUBLANES, :]
    rb = min(rows, 128)
    for s in range(conv_dim // LANES):
        sl = slice(s * LANES, (s + 1) * LANES)
        for r0 in reversed(range(0, rows, rb)):
            base = xpad_ref[r0:r0 + rb + SUBLANES, sl]
            acc = cb_ref[:, sl] + cw_ref[3:4, sl] * base[SUBLANES:SUBLANES + rb, :]
            for k in range(3):
                win = pltpu.roll(base, rb + 3 - k, axis=0)
                acc = acc + cw_ref[k:k + 1, sl] * win[0:rb, :]
            xpad_ref[SUBLANES + r0:SUBLANES + r0 + rb, sl] = _silu(acc)
        if s % conv_tick == conv_tick - 1:
            drip.tick()
    xpad_ref[0:SUBLANES, :] = carry

    dt_all = _softplus(dt_raw + dtb_ref[...])
    if n_pad:
        prow = jax.lax.broadcasted_iota(jnp.int32, dt_all.shape, 0)
        dt_all = jnp.where(prow >= n_pad, dt_all, 0.0)
    a = -jnp.exp(alog_ref[...])

    ti = jax.lax.broadcasted_iota(jnp.int32, (sub, sub), 0)
    si = jax.lax.broadcasted_iota(jnp.int32, (sub, sub), 1)
    causal = ti >= si
    lane_g = jax.lax.broadcasted_iota(jnp.int32, (sub, gw), 1)
    lane_1 = jax.lax.broadcasted_iota(jnp.int32, (1, gw), 1)

    def block_decays(u):
        dt = dt_all[u * sub:(u + 1) * sub, :]
        cs = _cumsum_rows(dt * a)
        cs_last = cs[sub - 1:sub, :]
        return (cs, cs.T, dt.T,
                jnp.exp(cs),
                dt * jnp.exp(cs_last - cs),
                jnp.exp(cs_last))

    def group_bc(u, g):
        rs = slice(SUBLANES + u * sub, SUBLANES + (u + 1) * sub)
        bg_t = xpad_ref[rs, d_inner + g * n:d_inner + (g + 1) * n].T.astype(BF16)
        c_lo = d_inner + SSD_GROUPS * n + g * n
        cg_b = xpad_ref[rs, c_lo:c_lo + n].astype(BF16)
        return bg_t, cg_b, jnp.dot(cg_b, bg_t, preferred_element_type=F32)

    blocks = range(rows // sub)
    decays = [block_decays(u) for u in blocks]
    bcs = [[group_bc(u, g) for g in range(SSD_GROUPS)] for u in blocks]
    for u in blocks:
        rs = slice(SUBLANES + u * sub, SUBLANES + (u + 1) * sub)
        us = slice(u * sub, (u + 1) * sub)
        cs, cs_t, dt_t, e_cs, w_end, e_last = decays[u]
        for g in range(SSD_GROUPS):
            gsl = slice(g * gw, (g + 1) * gw)
            hsl = slice(g * HEADS_PER_GROUP, (g + 1) * HEADS_PER_GROUP)
            xg = xpad_ref[rs, gsl]
            xg_b = xg.astype(BF16)
            bg_t, cg_b, cb = bcs[u][g]
            st = st_ref[g]
            y = jnp.dot(cg_b, st.astype(BF16), preferred_element_type=F32) * _expand_heads4(e_cs[:, hsl], lane_g)
            ms, xms = [], []
            for j in range(HEADS_PER_GROUP):
                h = g * HEADS_PER_GROUP + j
                seg = cs[:, h:h + 1] - cs_t[h:h + 1, :]
                m = cb * jnp.exp(jnp.where(causal, seg, -jnp.inf)) * dt_t[h:h + 1, :]
                in_head = (lane_g >= j * SSD_HEAD_DIM) & (lane_g < (j + 1) * SSD_HEAD_DIM)
                ms.append(m.astype(BF16))
                xms.append(jnp.where(in_head, xg_b, jnp.zeros_like(xg_b)))
            y = y + jnp.dot(jnp.concatenate(ms, axis=1), jnp.concatenate(xms, axis=0), preferred_element_type=F32)
            y = y + dexp_ref[:, gsl] * xg
            wg = (xg * _expand_heads4(w_end[:, hsl], lane_g)).astype(BF16)
            st_ref[g] = st * _expand_heads4(e_last[:, hsl], lane_1) + jnp.dot(bg_t, wg, preferred_element_type=F32)
            y = y * _silu(z_ref[us, gsl])
            y_ref[us, gsl] = _rms_rows(y, nw_ref[:, gsl]).astype(BF16)
            drip.tick()

    drip.flush()
    xmid_ref[...] = x + jnp.dot(y_ref[...], wout_ref[...], preferred_element_type=F32)

    @pl.when((c == per_seq - 1) & (t < tiles))
    def _():
        for g in range(SSD_GROUPS):
            sn_ref[0, g * gw:(g + 1) * gw, :] = st_ref[g].T
        cn_ref[0] = carry


def ssd_layer(x, mix_w, ffn_w, s0, c0, nb, seq, n_pad, final_norm):
    rows = min(SSD_CHUNK, seq)
    assert rows % SSD_SUBCHUNK == 0
    d = x.shape[1]
    w_in, conv_w, norm_w, w_out = mix_w[1], mix_w[2], mix_w[7], mix_w[8]
    d_inner = norm_w.shape[1]
    conv_dim = conv_w.shape[1]
    dff = ffn_w[1].shape[1]
    hp = s0.shape[1]
    per_seq = seq // rows
    tiles = nb * per_seq
    shared0 = s0.shape[0] == 1
    assert w_in.shape[1] == d_inner + conv_dim + LANES
    x_spec, o_spec = _lag_specs(tiles, rows, d)
    return pl.pallas_call(
        functools.partial(_ssd_layer_kernel, rows=rows, sub=SSD_SUBCHUNK, n_pad=n_pad, tiles=tiles, per_seq=per_seq,
                          final_norm=final_norm),
        grid=(tiles + 1,),
        in_specs=[x_spec] + _resident_all(mix_w) + [
            _seq_init_spec((hp, SSD_D_STATE), shared0, tiles, per_seq),
            _seq_init_spec((SUBLANES, conv_dim), shared0, tiles, per_seq),
        ] + _resident_all(ffn_w),
        out_specs=[
            o_spec,
            _seq_out_spec((hp, SSD_D_STATE), tiles, per_seq),
            _seq_out_spec((SUBLANES, conv_dim), tiles, per_seq),
        ],
        out_shape=[
            jax.ShapeDtypeStruct((nb * seq, d), F32),
            jax.ShapeDtypeStruct((nb, hp, SSD_D_STATE), F32),
            jax.ShapeDtypeStruct((nb, SUBLANES, conv_dim), F32),
        ],
        scratch_shapes=[
            pltpu.VMEM((rows, d), F32),
            pltpu.VMEM((rows, d), F32),
            pltpu.VMEM((SSD_GROUPS, SSD_D_STATE, GROUP_WIDTH), F32),
            pltpu.VMEM((rows + SUBLANES, conv_dim), F32),
            pltpu.VMEM((rows, d_inner), F32),
            pltpu.VMEM((rows, d_inner), BF16),
        ],
        compiler_params=_cparams("arbitrary", vmem=SSD_LAYER_VMEM_BYTES),
        name="ssd_layer",
    )(x, *_operands(mix_w), s0, c0, *_operands(ffn_w))


def _sc_layer_kernel(x_ref, gm_ref, win_ref, cw_ref, wout_ref, c0_ref, gf_ref, wg_ref, wu_ref, wd_ref, gl_ref,
                     o_ref, cn_ref, xmid_ref, acc_ref, ppad_ref, *, rows, tiles, per_seq, final_norm):
    t = pl.program_id(0)
    c = jax.lax.rem(t, per_seq)

    @pl.when(t == 0)
    def _():
        xmid_ref[...] = jnp.zeros_like(xmid_ref)

    @pl.when(c == 0)
    def _():
        ppad_ref[0:SUBLANES, :] = c0_ref[0]

    for step in _ffn_steps(xmid_ref, gf_ref, wg_ref, wu_ref, wd_ref, gl_ref, o_ref, acc_ref, final_norm):
        step()

    x = x_ref[...]
    d = x.shape[1]
    hn = _rms_rows(x, gm_ref[...]).astype(BF16)
    p = (jnp.dot(hn, win_ref[:, d:2 * d], preferred_element_type=F32)
         * jnp.dot(hn, win_ref[:, 2 * d:3 * d], preferred_element_type=F32))
    ppad_ref[SUBLANES:SUBLANES + rows, :] = p
    v = (cw_ref[0:1, :] * ppad_ref[SUBLANES - 2:SUBLANES - 2 + rows, :]
         + cw_ref[1:2, :] * ppad_ref[SUBLANES - 1:SUBLANES - 1 + rows, :]
         + cw_ref[2:3, :] * p)
    h = (jnp.dot(hn, win_ref[:, 0:d], preferred_element_type=F32) * v).astype(BF16)
    xmid_ref[...] = x + jnp.dot(h, wout_ref[...], preferred_element_type=F32)
    carry = ppad_ref[rows:rows + SUBLANES, :]
    ppad_ref[0:SUBLANES, :] = carry

    @pl.when((c == per_seq - 1) & (t < tiles))
    def _():
        cn_ref[0] = carry


def sc_layer(x, mix_w, ffn_w, c0, nb, seq, rows, final_norm):
    d = x.shape[1]
    dff = ffn_w[1].shape[1]
    per_seq = seq // rows
    tiles = nb * per_seq
    x_spec, o_spec = _lag_specs(tiles, rows, d)
    return pl.pallas_call(
        functools.partial(_sc_layer_kernel, rows=rows, tiles=tiles, per_seq=per_seq, final_norm=final_norm),
        grid=(tiles + 1,),
        in_specs=[x_spec] + _resident_all(mix_w) + [
            _seq_init_spec((SUBLANES, d), c0.shape[0] == 1, tiles, per_seq),
        ] + _resident_all(ffn_w),
        out_specs=[o_spec, _seq_out_spec((SUBLANES, d), tiles, per_seq)],
        out_shape=[jax.ShapeDtypeStruct((nb * seq, d), F32), jax.ShapeDtypeStruct((nb, SUBLANES, d), F32)],
        scratch_shapes=[pltpu.VMEM((rows, d), F32), pltpu.VMEM((rows, d), F32), pltpu.VMEM((rows + SUBLANES, d), F32)],
        compiler_params=_cparams("arbitrary"),
        name="sc_layer",
    )(x, *_operands(mix_w), c0, *_operands(ffn_w))


def _cf_layer_kernel(x_ref, gm_ref, w1_ref, b1_ref, dw_ref, db_ref, lg_ref, lb_ref, w2_ref, b2_ref, c0_ref,
                     gf_ref, wg_ref, wu_ref, wd_ref, gl_ref,
                     o_ref, cn_ref, xmid_ref, acc_ref, upad_ref, v_ref, *, rows, tiles, per_seq, final_norm):
    t = pl.program_id(0)
    c = jax.lax.rem(t, per_seq)
    taps = dw_ref.shape[0]
    hist = CF_CARRY_ROWS
    first = hist - (taps - 1)

    @pl.when(t == 0)
    def _():
        xmid_ref[...] = jnp.zeros_like(xmid_ref)

    @pl.when(c == 0)
    def _():
        upad_ref[0:hist, :] = c0_ref[0]

    x = x_ref[...]
    d = x.shape[1]
    rb = min(rows, 128)
    drip = _Drip(_ffn_steps(xmid_ref, gf_ref, wg_ref, wu_ref, wd_ref, gl_ref, o_ref, acc_ref, final_norm),
                 (d // LANES) * (rows // rb))
    hn = _rms_rows(x, gm_ref[...]).astype(BF16)
    u = ((jnp.dot(hn, w1_ref[:, 0:d], preferred_element_type=F32) + b1_ref[:, 0:d])
         * _sigmoid(jnp.dot(hn, w1_ref[:, d:2 * d], preferred_element_type=F32) + b1_ref[:, d:2 * d]))
    upad_ref[hist:hist + rows, :] = u
    for s in range(d // LANES):
        sl = slice(s * LANES, (s + 1) * LANES)
        for r0 in range(0, rows, rb):
            base = upad_ref[r0:r0 + rb + hist, sl]
            acc = jnp.broadcast_to(db_ref[:, sl], (rb, LANES))
            for phase in range(SUBLANES):
                steps = [q for q in range(hist // SUBLANES + 1) if first <= q * SUBLANES + phase <= hist]
                win = base if phase == 0 else pltpu.roll(base, rb + hist - phase, axis=0)
                for q in steps:
                    k = q * SUBLANES + phase - first
                    acc = acc + dw_ref[k:k + 1, sl] * win[q * SUBLANES:q * SUBLANES + rb, :]
            v_ref[r0:r0 + rb, sl] = acc
            drip.tick()
    drip.flush()
    v = v_ref[...]
    mu = jnp.mean(v, axis=-1, keepdims=True)
    dv = v - mu
    var = jnp.mean(dv * dv, axis=-1, keepdims=True)
    y = dv * jax.lax.rsqrt(var + LN_EPS) * lg_ref[...] + lb_ref[...]
    h = _silu(y).astype(BF16)
    xmid_ref[...] = x + b2_ref[...] + jnp.dot(h, w2_ref[...], preferred_element_type=F32)
    carry = upad_ref[rows:rows + hist, :]
    upad_ref[0:hist, :] = carry

    @pl.when((c == per_seq - 1) & (t < tiles))
    def _():
        cn_ref[0] = carry


def cf_layer(x, mix_w, ffn_w, c0, nb, seq, rows, final_norm):
    d = x.shape[1]
    dff = ffn_w[1].shape[1]
    per_seq = seq // rows
    tiles = nb * per_seq
    assert mix_w[3].shape[0] - 1 <= CF_CARRY_ROWS
    x_spec, o_spec = _lag_specs(tiles, rows, d)
    return pl.pallas_call(
        functools.partial(_cf_layer_kernel, rows=rows, tiles=tiles, per_seq=per_seq, final_norm=final_norm),
        grid=(tiles + 1,),
        in_specs=[x_spec] + _resident_all(mix_w) + [
            _seq_init_spec((CF_CARRY_ROWS, d), c0.shape[0] == 1, tiles, per_seq),
        ] + _resident_all(ffn_w),
        out_specs=[o_spec, _seq_out_spec((CF_CARRY_ROWS, d), tiles, per_seq)],
        out_shape=[jax.ShapeDtypeStruct((nb * seq, d), F32), jax.ShapeDtypeStruct((nb, CF_CARRY_ROWS, d), F32)],
        scratch_shapes=[pltpu.VMEM((rows, d), F32), pltpu.VMEM((rows, d), F32),
                        pltpu.VMEM((rows + CF_CARRY_ROWS, d), F32), pltpu.VMEM((rows, d), F32)],
        compiler_params=_cparams("arbitrary"),
        name="cf_layer",
    )(x, *_operands(mix_w), c0, *_operands(ffn_w))


def _expand_heads_mxu(v, d_inner):
    hrow = jax.lax.broadcasted_iota(jnp.int32, (LANES, d_inner), 0)
    col = jax.lax.broadcasted_iota(jnp.int32, (LANES, d_inner), 1)
    onehot = jnp.where((col >= hrow * SSD_HEAD_DIM) & (col < (hrow + 1) * SSD_HEAD_DIM), 1.0, 0.0).astype(BF16)
    hi = v.astype(BF16)
    r1 = v - hi.astype(F32)
    mid = r1.astype(BF16)
    lo = (r1 - mid.astype(F32)).astype(BF16)
    out = jnp.dot(hi, onehot, preferred_element_type=F32)
    out = out + jnp.dot(mid, onehot, preferred_element_type=F32)
    return out + jnp.dot(lo, onehot, preferred_element_type=F32)


def _ssd_dec_pre_kernel(xa_ref, xb_ref, dt_ref, cbuf_ref, cw_ref, cb_ref, dtb_ref, alog_ref,
                        xs_ref, xdt_t_ref, b_ref, c_ref, da_ref, cnew_ref):
    conv_dim = cw_ref.shape[1]
    d_inner = xs_ref.shape[1]
    gn = b_ref.shape[1]

    def conv_silu(xnew, lo, hi):
        acc = cb_ref[:, lo:hi] + cw_ref[3:4, lo:hi] * xnew
        for k in range(3):
            acc = acc + cw_ref[k:k + 1, lo:hi] * cbuf_ref[:, k * conv_dim + lo:k * conv_dim + hi]
        return _silu(acc)

    xa = xa_ref[...]
    xb = xb_ref[...]
    cnew_ref[:, 0:2 * conv_dim] = cbuf_ref[:, conv_dim:3 * conv_dim]
    cnew_ref[:, 2 * conv_dim:2 * conv_dim + d_inner] = xa
    cnew_ref[:, 2 * conv_dim + d_inner:3 * conv_dim] = xb
    dt = _softplus(dt_ref[...] + dtb_ref[...])
    da_ref[...] = jnp.exp(dt * -jnp.exp(alog_ref[...]))
    xs = conv_silu(xa, 0, d_inner)
    xs_ref[...] = xs
    xdt_t_ref[...] = (xs * _expand_heads_mxu(dt, d_inner)).T.astype(BF16)
    bc = conv_silu(xb, d_inner, conv_dim)
    b_ref[...] = bc[:, 0:gn]
    c_ref[...] = bc[:, gn:2 * gn]


def ssd_dec_pre(zx, cbuf, conv_w, conv_b, dt_bias, a_log, d_inner):
    nb = zx.shape[0]
    conv_dim = conv_w.shape[1]
    assert conv_dim == 2 * d_inner
    gn = (conv_dim - d_inner) // 2
    dt_blk = (d_inner + conv_dim) // LANES

    def full(shape):
        return pl.BlockSpec(shape, lambda i: (0, 0))

    return pl.pallas_call(
        _ssd_dec_pre_kernel,
        grid=(1,),
        in_specs=[
            pl.BlockSpec((nb, d_inner), lambda i: (0, 1)),
            pl.BlockSpec((nb, d_inner), lambda i: (0, 2)),
            pl.BlockSpec((nb, LANES), lambda i: (0, dt_blk)),
            full((nb, 3 * conv_dim)), full((4, conv_dim)), full((1, conv_dim)), full((1, LANES)), full((1, LANES)),
        ],
        out_specs=[
            full((nb, d_inner)), full((d_inner, nb)), full((nb, gn)), full((nb, gn)), full((nb, LANES)),
            full((nb, 3 * conv_dim)),
        ],
        out_shape=[
            jax.ShapeDtypeStruct((nb, d_inner), F32), jax.ShapeDtypeStruct((d_inner, nb), BF16),
            jax.ShapeDtypeStruct((nb, gn), F32), jax.ShapeDtypeStruct((nb, gn), F32),
            jax.ShapeDtypeStruct((nb, LANES), F32), jax.ShapeDtypeStruct((nb, 3 * conv_dim), F32),
        ],
        compiler_params=_cparams("arbitrary"),
        name="ssd_dec_pre",
    )(zx, zx, zx, cbuf, conv_w, conv_b, dt_bias, a_log)


def _ssd_dec_state_kernel(*refs, chained):
    if chained:
        refs = refs[1:]
    s_ref, xdt_t_ref, b_ref, c_ref, da_ref, sn_ref, y_ref = refs
    i = pl.program_id(0)
    bs = s_ref.shape[0]
    hp, nb = xdt_t_ref.shape
    n = SSD_D_STATE
    gw = GROUP_WIDTH
    groups = hp // gw
    c_blk = c_ref[...].astype(BF16)
    seq_row = jax.lax.broadcasted_iota(jnp.int32, (nb, n), 0)
    out_row = jax.lax.broadcasted_iota(jnp.int32, (bs, gw), 0)

    def one_seq(k, ys):
        da_row = da_ref[pl.ds(k, 1), :]
        new_ys = []
        for g in range(groups):
            b_sel = jnp.where(seq_row == i * bs + k, b_ref[:, g * n:(g + 1) * n], 0.0).astype(BF16)
            outer = jnp.dot(xdt_t_ref[g * gw:(g + 1) * gw, :], b_sel, preferred_element_type=F32)
            parts = []
            for j in range(HEADS_PER_GROUP):
                h = g * HEADS_PER_GROUP + j
                hsl = slice(g * gw + j * SSD_HEAD_DIM, g * gw + (j + 1) * SSD_HEAD_DIM)
                snew = s_ref[k, hsl, :] * da_row[:, h:h + 1] + outer[j * SSD_HEAD_DIM:(j + 1) * SSD_HEAD_DIM, :]
                sn_ref[k, hsl, :] = snew
                parts.append(snew.astype(BF16))
            y8 = jax.lax.dot_general(c_blk[:, g * n:(g + 1) * n], jnp.concatenate(parts, axis=0),
                                     (((1,), (1,)), ((), ())), preferred_element_type=F32)
            new_ys.append(jnp.where(out_row == k, y8, ys[g]))
        return tuple(new_ys)

    ys = jax.lax.fori_loop(0, bs, one_seq, tuple(jnp.zeros((bs, gw), F32) for _ in range(groups)))
    for g in range(groups):
        y_ref[:, g * gw:(g + 1) * gw] = ys[g]


def ssd_dec_state(states, layer, prev, xdt_t, bm, cm, da):
    n_layers, nb, hp, n = states.shape
    bs = DEC_SEQ_BLOCK
    chained = prev is not None

    def full(shape):
        return pl.BlockSpec(shape, lambda i: (0, 0))

    state_blk = pl.BlockSpec((None, bs, hp, n), lambda i: (layer, i, 0, 0))
    in_specs = [state_blk, full((hp, nb)), full(bm.shape),
                pl.BlockSpec((bs, cm.shape[1]), lambda i: (i, 0)), pl.BlockSpec((bs, LANES), lambda i: (i, 0))]
    args = [states, xdt_t, bm, cm, da]
    if chained:
        in_specs = [pl.BlockSpec(memory_space=pl.ANY)] + in_specs
        args = [prev] + args
    return pl.pallas_call(
        functools.partial(_ssd_dec_state_kernel, chained=chained),
        grid=(nb // bs,),
        in_specs=in_specs,
        out_specs=[state_blk, pl.BlockSpec((bs, hp), lambda i: (i, 0))],
        out_shape=[jax.ShapeDtypeStruct(states.shape, F32), jax.ShapeDtypeStruct((nb, hp), F32)],
        input_output_aliases={0: 0} if chained else {},
        compiler_params=_cparams("arbitrary"),
        name="ssd_dec_state",
    )(*args)


def _ssd_dec_post_kernel(y_ref, xs_ref, z_ref, dexp_ref, nw_ref, h_ref):
    gw = GROUP_WIDTH
    y = (y_ref[...] + dexp_ref[...] * xs_ref[...]) * _silu(z_ref[...])
    for g in range(y.shape[1] // gw):
        gsl = slice(g * gw, (g + 1) * gw)
        h_ref[:, gsl] = _rms_rows(y[:, gsl], nw_ref[:, gsl]).astype(BF16)


def ssd_dec_post(y, xs, zx, d_exp, norm_w):
    nb, d_inner = y.shape

    def full(shape):
        return pl.BlockSpec(shape, lambda i: (0, 0))

    return pl.pallas_call(
        _ssd_dec_post_kernel,
        grid=(1,),
        in_specs=[full((nb, d_inner)), full((nb, d_inner)), full((nb, d_inner)), full((1, d_inner)), full((1, d_inner))],
        out_specs=full((nb, d_inner)),
        out_shape=jax.ShapeDtypeStruct((nb, d_inner), BF16),
        compiler_params=_cparams("arbitrary"),
        name="ssd_dec_post",
    )(y, xs, zx, d_exp, norm_w)


def _sc_dec_kernel(bg_ref, cg_ref, xv_ref, w_ref, buf_ref, h_ref, bn_ref):
    d = w_ref.shape[1]
    p = cg_ref[...] * xv_ref[...]
    v = w_ref[0:1, :] * buf_ref[:, 0:d] + w_ref[1:2, :] * buf_ref[:, d:2 * d] + w_ref[2:3, :] * p
    h_ref[...] = (bg_ref[...] * v).astype(BF16)
    bn_ref[:, 0:d] = buf_ref[:, d:2 * d]
    bn_ref[:, d:2 * d] = p


def sc_dec(bcx, conv_w, buf):
    nb = bcx.shape[0]
    d = conv_w.shape[1]

    def full(shape, col=0):
        return pl.BlockSpec(shape, lambda i: (0, col))

    return pl.pallas_call(
        _sc_dec_kernel,
        grid=(1,),
        in_specs=[full((nb, d), 0), full((nb, d), 1), full((nb, d), 2), full((3, d)), full((nb, 2 * d))],
        out_specs=[full((nb, d)), full((nb, 2 * d))],
        out_shape=[jax.ShapeDtypeStruct((nb, d), BF16), jax.ShapeDtypeStruct((nb, 2 * d), F32)],
        compiler_params=_cparams("arbitrary"),
        name="sc_dec",
    )(bcx, bcx, bcx, conv_w, buf)


def _cf_dec_kernel(a1_ref, a2_ref, w_ref, b_ref, lg_ref, lb_ref, buf_ref, h_ref, bn_ref):
    taps, d = w_ref.shape
    u = a1_ref[...] * _sigmoid(a2_ref[...])
    v = b_ref[...] + w_ref[taps - 1:taps, :] * u
    for k in range(taps - 1):
        v = v + w_ref[k:k + 1, :] * buf_ref[:, k * d:(k + 1) * d]
    mu = jnp.mean(v, axis=-1, keepdims=True)
    dv = v - mu
    var = jnp.mean(dv * dv, axis=-1, keepdims=True)
    y = dv * jax.lax.rsqrt(var + LN_EPS) * lg_ref[...] + lb_ref[...]
    h_ref[...] = _silu(y).astype(BF16)
    bn_ref[:, 0:(taps - 2) * d] = buf_ref[:, d:(taps - 1) * d]
    bn_ref[:, (taps - 2) * d:(taps - 1) * d] = u


def cf_dec(a, dw_w, dw_b, ln_g, ln_b, buf):
    nb = a.shape[0]
    taps, d = dw_w.shape

    def full(shape, col=0):
        return pl.BlockSpec(shape, lambda i: (0, col))

    return pl.pallas_call(
        _cf_dec_kernel,
        grid=(1,),
        in_specs=[full((nb, d), 0), full((nb, d), 1), full((taps, d)), full((1, d)), full((1, d)), full((1, d)),
                  full((nb, (taps - 1) * d))],
        out_specs=[full((nb, d)), full((nb, (taps - 1) * d))],
        out_shape=[jax.ShapeDtypeStruct((nb, d), BF16), jax.ShapeDtypeStruct((nb, (taps - 1) * d), F32)],
        compiler_params=_cparams("arbitrary"),
        name="cf_dec",
    )(a, a, dw_w, dw_b, ln_g, ln_b, buf)


def kernel(x_prompt, x_sample, state_ssm, state_ssm_conv, state_sconv, state_cconv, meta_tokens,
           norm_mix, norm_ffn, norm_final, ssd_w_in, ssd_conv_w, ssd_conv_b, ssd_dt_bias, ssd_a_log,
           ssd_d, ssd_norm_w, ssd_w_out, sc_w_in, sc_conv_w, sc_w_out, cf_w_pw1, cf_b_pw1, cf_dw_w,
           cf_dw_b, cf_ln_g, cf_ln_b, cf_w_pw2, cf_b_pw2, ffn_w_gate, ffn_w_up, ffn_w_down):
    nb, seq, d = x_prompt.shape
    nd = x_sample.shape[0]
    assert x_sample.shape[1] == 1
    depth = norm_mix.shape[0]
    n_meta = meta_tokens.shape[0]
    heads = ssd_a_log.shape[1]
    d_inner = ssd_norm_w.shape[1]
    conv_dim = ssd_conv_w.shape[2]
    hp = heads * SSD_HEAD_DIM
    assert hp == d_inner and heads <= LANES and n_meta <= SSD_SUBCHUNK
    assert seq % SSD_CHUNK == 0 and seq % LAYER_ROWS == 0
    ssd_n = d_inner + conv_dim + LANES

    x_main = x_prompt.reshape(nb * seq, d)
    x_meta = meta_tokens.astype(F32)
    x_dec = x_sample.reshape(nd, d)
    zeros_d = jnp.zeros((1, d), F32)
    states = state_ssm.reshape(state_ssm.shape[0], nd, hp, SSD_D_STATE)
    dec_states = None

    def row(v):
        return v.reshape(1, -1).astype(F32)

    def pad_lanes(v):
        return jnp.pad(v.reshape(1, -1).astype(F32), ((0, 0), (0, LANES - v.shape[-1])))

    p_ssm, p_ssm_conv, p_sconv, p_cconv = [], [], [], []
    s_ssm_conv, s_sconv, s_cconv = [], [], []

    w_gate_all, w_up_all, w_down_all = ffn_w_gate.astype(BF16), ffn_w_up.astype(BF16), ffn_w_down.astype(BF16)
    ssd_w_in_all = jnp.pad(ssd_w_in, ((0, 0), (0, 0), (0, ssd_n - ssd_w_in.shape[2]))).astype(BF16)
    ssd_w_out_all = ssd_w_out.astype(BF16)

    for i in range(depth):
        kind, j = i % 3, i // 3
        last = i == depth - 1
        g_mix = row(norm_mix[i])
        ffn_w = (row(norm_ffn[i]), _Stacked(w_gate_all, i), _Stacked(w_up_all, i), _Stacked(w_down_all, i),
                 row(norm_final))
        if kind == 0:
            w_in = _Stacked(ssd_w_in_all, j)
            conv_w, conv_b = ssd_conv_w[j], row(ssd_conv_b[j])
            dt_bias, a_log = pad_lanes(ssd_dt_bias[j]), pad_lanes(ssd_a_log[j])
            d_exp = jnp.repeat(ssd_d[j].astype(F32), SSD_HEAD_DIM).reshape(1, d_inner)
            norm_w = row(ssd_norm_w[j])
            w_out, b_out = _Stacked(ssd_w_out_all, j), zeros_d
            mix_w = (g_mix, w_in, conv_w, conv_b, dt_bias, a_log, d_exp, norm_w, w_out)
            zx_d = norm_matmul(x_dec, g_mix, w_in, jnp.zeros((1, ssd_n), F32), nd, _col_tile(ssd_n, 1024))
            cbuf = state_ssm_conv[j].reshape(nd, 3 * conv_dim)
            xs, xdt_t, bm, cm, da, cnew = ssd_dec_pre(zx_d, cbuf, conv_w, conv_b, dt_bias, a_log, d_inner)
            dec_states, y_d = ssd_dec_state(states, j, dec_states, xdt_t, bm, cm, da)
            h_d = ssd_dec_post(y_d, xs, zx_d, d_exp, norm_w)
            s_ssm_conv.append(cnew.reshape(nd, 3, conv_dim))
            x_pad = jnp.pad(x_meta, ((SSD_SUBCHUNK - n_meta, 0), (0, 0)))
            x_pad, st_m, ct_m = ssd_layer(x_pad, mix_w, ffn_w, jnp.zeros((1, hp, SSD_D_STATE), F32),
                                          jnp.zeros((1, SUBLANES, conv_dim), F32), 1, SSD_SUBCHUNK,
                                          SSD_SUBCHUNK - n_meta, False)
            x_meta = x_pad[SSD_SUBCHUNK - n_meta:]
            x_main, st_p, ct_p = ssd_layer(x_main, mix_w, ffn_w, st_m, ct_m, nb, seq, 0, last)
            p_ssm.append(st_p.reshape((nb,) + state_ssm.shape[2:]))
            p_ssm_conv.append(ct_p[:, SUBLANES - 3:, :])
        elif kind == 1:
            w_in = sc_w_in[j].astype(BF16)
            conv_w = sc_conv_w[j]
            w_out, b_out = sc_w_out[j].astype(BF16), zeros_d
            mix_w = (g_mix, w_in, conv_w, w_out)
            bcx_d = norm_matmul(x_dec, g_mix, w_in, jnp.zeros((1, 3 * d), F32), nd, 1024)
            h_d, bnew = sc_dec(bcx_d, conv_w, state_sconv[j].reshape(nd, 2 * d))
            s_sconv.append(bnew.reshape(nd, 2, d))
            x_meta, ct_m = sc_layer(x_meta, mix_w, ffn_w, jnp.zeros((1, SUBLANES, d), F32), 1, n_meta, n_meta, False)
            x_main, ct_p = sc_layer(x_main, mix_w, ffn_w, ct_m, nb, seq, LAYER_ROWS, last)
            p_sconv.append(ct_p[:, SUBLANES - 2:, :])
        else:
            w_in = cf_w_pw1[j].astype(BF16)
            dw_w = cf_dw_w[j]
            taps = dw_w.shape[0]
            w_out, b_out = cf_w_pw2[j].astype(BF16), row(cf_b_pw2[j])
            mix_w = (g_mix, w_in, row(cf_b_pw1[j]), dw_w, row(cf_dw_b[j]), row(cf_ln_g[j]), row(cf_ln_b[j]),
                     w_out, b_out)
            a_d = norm_matmul(x_dec, g_mix, w_in, row(cf_b_pw1[j]), nd, 1024)
            h_d, bnew = cf_dec(a_d, dw_w, row(cf_dw_b[j]), row(cf_ln_g[j]), row(cf_ln_b[j]),
                               state_cconv[j].reshape(nd, (taps - 1) * d))
            s_cconv.append(bnew.reshape(nd, taps - 1, d))
            x_meta, ct_m = cf_layer(x_meta, mix_w, ffn_w, jnp.zeros((1, CF_CARRY_ROWS, d), F32), 1, n_meta, n_meta,
                                    False)
            x_main, ct_p = cf_layer(x_main, mix_w, ffn_w, ct_m, nb, seq, LAYER_ROWS, last)
            p_cconv.append(ct_p[:, CF_CARRY_ROWS - (taps - 1):, :])
        x_dec = ffn(matmul_res(h_d, w_out, b_out, x_dec, nd), ffn_w, nd, last)

    y_prompt = x_main.reshape(nb, seq, d)
    y_sample = x_dec.reshape(nd, 1, d)
    s_ssm = dec_states.reshape(state_ssm.shape)
    return (y_prompt, y_sample, jnp.stack(p_ssm), jnp.stack(p_ssm_conv), jnp.stack(p_sconv), jnp.stack(p_cconv),
            s_ssm, jnp.stack(s_ssm_conv), jnp.stack(s_sconv), jnp.stack(s_cconv))
```
